```python
import math
import jax, jax.numpy as jnp
from jax import lax
import numpy as np

D_MODEL = 1024
BATCH = 8
SEQ = 4096
DEPTH = 2

HEAD_DIM = 64
ATT_HEADS = 6
CONV_GROUPS = 4
RWKV_HEADS = 6
ATT_W = ATT_HEADS * HEAD_DIM
CONV_W = CONV_GROUPS * HEAD_DIM
RWKV_W = RWKV_HEADS * HEAD_DIM
D_MIX = ATT_W + CONV_W + RWKV_W
DIFF_HALF = HEAD_DIM // 2
CONV_K = 3
DECAY_LORA = 64
ICLR_LORA = 64
GATE_LORA = 128
RWKV_SPLITS = (RWKV_W, RWKV_W, RWKV_W, DECAY_LORA, ICLR_LORA, GATE_LORA)
RWKV_COLS = 3 * RWKV_W + DECAY_LORA + ICLR_LORA + GATE_LORA
N_IN = 3 * ATT_W + 3 * CONV_W + RWKV_COLS
D_FF = 4 * D_MODEL
Q_BLOCK = 128
NORM_EPS = 1e-6
GN_EPS = 64e-5

kernel_name = 'hymba_diffattn_shortconv_rwkv7'


def _split_points(sizes):
    return [int(s) for s in np.cumsum(sizes)[:-1]]


def rms_norm(x, g):
    xf = x.astype(jnp.float32)
    y = xf * lax.rsqrt(jnp.mean(xf * xf, axis=-1, keepdims=True) + NORM_EPS)
    return (y * g.astype(jnp.float32)).astype(x.dtype)


def diff_attention(q, k, v, lam, subln_g, lambda_init):
    bsz, seq, _ = q.shape
    nb = seq // Q_BLOCK
    q = q.astype(jnp.float32).reshape(bsz, seq, ATT_HEADS, 2, DIFF_HALF).transpose(3, 0, 2, 1, 4)
    k = k.astype(jnp.float32).reshape(bsz, seq, ATT_HEADS, 2, DIFF_HALF).transpose(3, 0, 2, 1, 4)
    v = v.astype(jnp.float32).reshape(bsz, seq, ATT_HEADS, HEAD_DIM).transpose(0, 2, 1, 3)
    q_blocks = q.reshape(2, bsz, ATT_HEADS, nb, Q_BLOCK, DIFF_HALF).transpose(3, 0, 1, 2, 4, 5)
    key_pos = jnp.arange(seq)
    lam = lam.astype(jnp.float32)
    scale = DIFF_HALF ** -0.5

    def one_block(args):
        q_blk, blk = args
        q_pos = blk * Q_BLOCK + jnp.arange(Q_BLOCK)
        causal = key_pos[None, :] <= q_pos[:, None]
        s = jnp.einsum('nbhqd,nbhkd->nbhqk', q_blk, k) * scale
        p = jax.nn.softmax(jnp.where(causal, s, -jnp.inf), axis=-1)
        return jnp.einsum('bhqk,bhkd->bhqd', p[0] - lam * p[1], v)

    o = lax.map(one_block, (q_blocks, jnp.arange(nb)))
    o = o.transpose(1, 0, 3, 2, 4).reshape(bsz, seq, ATT_HEADS, HEAD_DIM)
    o = o * lax.rsqrt(jnp.mean(o * o, axis=-1, keepdims=True) + NORM_EPS)
    o = o * subln_g.astype(jnp.float32) * (1.0 - lambda_init)
    return o.reshape(bsz, seq, ATT_W)


def short_conv(b, c, u, conv_w):
    seq = u.shape[1]
    z = c * u
    zp = jnp.pad(z, ((0, 0), (CONV_K - 1, 0), (0, 0)))
    y = sum(conv_w[j] * zp[:, j:j + seq] for j in range(CONV_K))
    return b * y


def rwkv7_time_mix(p, shift_mu, w0, w_up, a0, a_up, g_up, k_k, k_a, r_k, lnx_g, lnx_b):
    f32 = jnp.float32
    bsz, seq, _ = p.shape
    p = p.astype(f32)
    p_prev = jnp.pad(p, ((0, 0), (1, 0), (0, 0)))[:, :seq]
    z = p + shift_mu.astype(f32) * (p_prev - p)
    r, k, v, w_dn, a_dn, g_dn = jnp.split(z, _split_points(RWKV_SPLITS), axis=-1)
    w = w0.astype(f32) + jnp.tanh(w_dn) @ w_up.astype(f32)
    decay = jnp.exp(-jnp.exp(-jax.nn.softplus(-w) - 0.5))
    a = jax.nn.sigmoid(a0.astype(f32) + a_dn @ a_up.astype(f32))
    g = jax.nn.sigmoid(g_dn) @ g_up.astype(f32)
    kk = k * k_k.astype(f32)
    k = k * (1.0 + (a - 1.0) * k_a.astype(f32))
    heads = lambda t: t.reshape(bsz, seq, RWKV_HEADS, HEAD_DIM)
    r, decay, k, v, kk, a = [heads(t) for t in (r, decay, k, v, kk, a)]
    kk = kk / jnp.maximum(jnp.sqrt(jnp.sum(kk * kk, axis=-1, keepdims=True)), 1e-12)

    def step(state, inp):
        r_t, w_t, k_t, v_t, kk_t, a_t = inp
        sa = jnp.einsum('bhvk,bhk->bhv', state, -kk_t)
        state = (state * w_t[:, :, None, :]
                 + sa[..., None] * (kk_t * a_t)[:, :, None, :]
                 + v_t[..., None] * k_t[:, :, None, :])
        y_t = jnp.einsum('bhvk,bhk->bhv', state, r_t)
        return state, y_t

    xs = tuple(t.transpose(1, 0, 2, 3) for t in (r, decay, k, v, kk, a))
    state0 = jnp.zeros((bsz, RWKV_HEADS, HEAD_DIM, HEAD_DIM), f32)
    _, y = lax.scan(step, state0, xs)
    y = y.transpose(1, 0, 2, 3)
    mu = jnp.mean(y, axis=-1, keepdims=True)
    var = jnp.mean(jnp.square(y - mu), axis=-1, keepdims=True)
    yn = (y - mu) * lax.rsqrt(var + GN_EPS)
    yn = yn * lnx_g.astype(f32).reshape(RWKV_HEADS, HEAD_DIM) + lnx_b.astype(f32).reshape(RWKV_HEADS, HEAD_DIM)
    bonus = jnp.sum(r * k * r_k.astype(f32), axis=-1, keepdims=True) * v
    return (yn + bonus).reshape(bsz, seq, RWKV_W) * g


def setup_inputs(seed: int = 0) -> dict:
    key = jax.random.key(seed)
    ks = iter(jax.random.split(key, 32))
    f32 = jnp.float32
    nrm = lambda shape, scale: jax.random.normal(next(ks), shape, f32) * scale
    uni = lambda shape: jax.random.uniform(next(ks), shape, f32)
    L = DEPTH
    return {
        'x': nrm((BATCH, SEQ, D_MODEL), 1.0),
        'norm_mix_g': 1.0 + nrm((L, D_MODEL), 0.02),
        'w_in': nrm((L, D_MODEL, N_IN), D_MODEL ** -0.5),
        'lam_q1': nrm((L, DIFF_HALF), 0.1),
        'lam_k1': nrm((L, DIFF_HALF), 0.1),
        'lam_q2': nrm((L, DIFF_HALF), 0.1),
        'lam_k2': nrm((L, DIFF_HALF), 0.1),
        'subln_g': 1.0 + nrm((L, HEAD_DIM), 0.02),
        'conv_w': nrm((L, CONV_K, CONV_W), CONV_K ** -0.5),
        'shift_mu': uni((L, RWKV_COLS)),
        'rwkv_w0': -6.0 + 5.0 * uni((L, RWKV_W)),
        'rwkv_w_up': nrm((L, DECAY_LORA, RWKV_W), 0.1 * DECAY_LORA ** -0.5),
        'rwkv_a0': nrm((L, RWKV_W), 0.1),
        'rwkv_a_up': nrm((L, ICLR_LORA, RWKV_W), 0.5 * ICLR_LORA ** -0.5),
        'rwkv_g_up': nrm((L, GATE_LORA, RWKV_W), GATE_LORA ** -0.5),
        'rwkv_k_k': 0.85 + nrm((L, RWKV_W), 0.02),
        'rwkv_k_a': 1.0 + nrm((L, RWKV_W), 0.02),
        'rwkv_r_k': nrm((L, RWKV_HEADS, HEAD_DIM), 0.1),
        'lnx_g': 1.0 + nrm((L, RWKV_W), 0.02),
        'lnx_b': nrm((L, RWKV_W), 0.02),
        'w_out': nrm((L, D_MIX, D_MODEL), D_MIX ** -0.5),
        'norm_mlp_g': 1.0 + nrm((L, D_MODEL), 0.02),
        'w_mlp_up': nrm((L, D_MODEL, D_FF), D_MODEL ** -0.5),
        'w_mlp_down': nrm((L, D_FF, D_MODEL), D_FF ** -0.5),
        'final_norm_g': 1.0 + nrm((D_MODEL,), 0.02),
    }


def reference(x, norm_mix_g, w_in, lam_q1, lam_k1, lam_q2, lam_k2, subln_g, conv_w,
              shift_mu, rwkv_w0, rwkv_w_up, rwkv_a0, rwkv_a_up, rwkv_g_up, rwkv_k_k,
              rwkv_k_a, rwkv_r_k, lnx_g, lnx_b, w_out, norm_mlp_g, w_mlp_up, w_mlp_down,
              final_norm_g):
    f32 = jnp.float32
    for l in range(DEPTH):
        h = rms_norm(x, norm_mix_g[l])
        proj = h @ w_in[l]
        p_att, p_conv, p_rwkv = jnp.split(proj, [3 * ATT_W, 3 * ATT_W + 3 * CONV_W], axis=-1)

        q, k, v = jnp.split(p_att, 3, axis=-1)
        lambda_init = 0.8 - 0.6 * math.exp(-0.3 * l)
        lam = (jnp.exp(jnp.sum(lam_q1[l].astype(f32) * lam_k1[l].astype(f32)))
               - jnp.exp(jnp.sum(lam_q2[l].astype(f32) * lam_k2[l].astype(f32)))
               + lambda_init)
        o_att = diff_attention(q, k, v, lam, subln_g[l], lambda_init)

        b_gate, c_gate, u = jnp.split(p_conv, 3, axis=-1)
        o_conv = short_conv(b_gate, c_gate, u, conv_w[l])

        o_rwkv = rwkv7_time_mix(p_rwkv, shift_mu[l], rwkv_w0[l], rwkv_w_up[l], rwkv_a0[l],
                                rwkv_a_up[l], rwkv_g_up[l], rwkv_k_k[l], rwkv_k_a[l],
                                rwkv_r_k[l], lnx_g[l], lnx_b[l])

        mixed = jnp.concatenate([o_att.astype(x.dtype), o_conv.astype(x.dtype),
                                 o_rwkv.astype(x.dtype)], axis=-1)
        x = x + mixed @ w_out[l]

        h = rms_norm(x, norm_mlp_g[l])
        x = x + jnp.square(jax.nn.relu(h @ w_mlp_up[l])) @ w_mlp_down[l]
    return rms_norm(x, final_norm_g)
```

```python
import functools
import math

import jax
import jax.numpy as jnp
from jax import lax
from jax.experimental import pallas as pl
from jax.experimental.pallas import tpu as pltpu

F32 = jnp.float32
BF16 = jnp.bfloat16

HEAD_DIM = 64
DIFF_HALF = HEAD_DIM // 2
ATT_HEADS = 6
CONV_GROUPS = 4
RWKV_HEADS = 6
ATT_W = ATT_HEADS * HEAD_DIM
CONV_W = CONV_GROUPS * HEAD_DIM
RWKV_W = RWKV_HEADS * HEAD_DIM
CONV_K = 3
DECAY_LORA = 64
ICLR_LORA = 64
GATE_LORA = 128
RWKV_COLS = 3 * RWKV_W + DECAY_LORA + ICLR_LORA + GATE_LORA
NORM_EPS = 1e-6
GN_EPS = 64e-5

LANES = 128
SUBLANES = 8
HEADS_PER_VREG = LANES // HEAD_DIM
VMEM_LIMIT_BYTES = 56 * 1024 * 1024

ROW_TILE = 512
ATT_TILE = 256
RWKV_CHUNK = 128
FF_CHUNK = 1024
NEG_BIG = -1e30


def _compiler_params(n_axes):
    return pltpu.CompilerParams(
        dimension_semantics=("arbitrary",) * n_axes,
        vmem_limit_bytes=VMEM_LIMIT_BYTES,
    )


def _const_spec(shape):
    nd = len(shape)
    return pl.BlockSpec(shape, lambda *_: (0,) * nd)


def _mm(a, b):
    return jnp.dot(a.astype(BF16), b.astype(BF16), preferred_element_type=F32)


def _mm_nt(a, b):
    return lax.dot_general(a.astype(BF16), b.astype(BF16), (((1,), (1,)), ((), ())),
                           preferred_element_type=F32)


def _mm_tn(a, b):
    return lax.dot_general(a.astype(BF16), b.astype(BF16), (((0,), (0,)), ((), ())),
                           preferred_element_type=F32)


def _mm_split(x, ones_bf16, terms):
    acc = None
    rem = x
    for t in range(terms):
        piece = rem.astype(BF16)
        d = jnp.dot(piece, ones_bf16, preferred_element_type=F32)
        acc = d if acc is None else acc + d
        if t + 1 < terms:
            rem = rem - piece.astype(F32)
    return acc


def _cumsum_rows(tril_bf16, x):
    acc = None
    rem = x
    for t in range(3):
        piece = rem.astype(BF16)
        d = jnp.dot(tril_bf16, piece, preferred_element_type=F32)
        acc = d if acc is None else acc + d
        if t < 2:
            rem = rem - piece.astype(F32)
    return acc


def _rms(x, g):
    ms = jnp.mean(x * x, axis=-1, keepdims=True)
    return x * lax.rsqrt(ms + NORM_EPS) * g


def _inproj_kernel(x_ref, g_ref, qscale_ref, w_att_ref, w_conv_ref, w_rw_ref,
                   att_ref, conv_ref, rw_ref):
    h = _rms(x_ref[...], g_ref[...]).astype(BF16)
    att = jnp.dot(h, w_att_ref[...], preferred_element_type=F32)
    att_ref[...] = (att * qscale_ref[...]).astype(BF16)
    conv_ref[...] = jnp.dot(h, w_conv_ref[...], preferred_element_type=F32)
    rw_ref[...] = jnp.dot(h, w_rw_ref[...], preferred_element_type=F32)


def _inproj(x2, g, qscale, w_att, w_conv, w_rw):
    t, d = x2.shape
    tm = ROW_TILE
    row = lambda n: pl.BlockSpec((tm, n), lambda i: (i, 0))
    return pl.pallas_call(
        _inproj_kernel,
        grid=(t // tm,),
        in_specs=[row(d), _const_spec(g.shape), _const_spec(qscale.shape),
                  _const_spec(w_att.shape), _const_spec(w_conv.shape), _const_spec(w_rw.shape)],
        out_specs=[row(3 * ATT_W), row(3 * CONV_W), row(RWKV_COLS)],
        out_shape=[jax.ShapeDtypeStruct((t, 3 * ATT_W), BF16),
                   jax.ShapeDtypeStruct((t, 3 * CONV_W), F32),
                   jax.ShapeDtypeStruct((t, RWKV_COLS), F32)],
        compiler_params=_compiler_params(1),
        name="inproj",
    )(x2, g, qscale, w_att, w_conv, w_rw)


def _attn_kernel(lam_ref, sg_ref, q_ref, k_ref, v_ref, o_ref, *, lambda_init):
    tq = q_ref.shape[1]
    i = pl.program_id(2)
    q = q_ref[0]
    lane = lax.broadcasted_iota(jnp.int32, (tq, LANES), 1)
    n_maps = LANES // DIFF_HALF
    q_sel = [jnp.where(lane // DIFF_HALF == c, q, jnp.zeros_like(q)) for c in range(n_maps)]
    r_idx = lax.broadcasted_iota(jnp.int32, (tq, tq), 0)
    c_idx = lax.broadcasted_iota(jnp.int32, (tq, tq), 1)
    causal = c_idx <= r_idx

    def step(j, carry, diagonal):
        start = pl.multiple_of(j * tq, tq)
        ks = k_ref[0, pl.ds(start, tq), :]
        vs = v_ref[0, pl.ds(start, tq), :]
        out = []
        for c in range(n_maps):
            m, l, acc = carry[c]
            s = lax.dot_general(q_sel[c], ks, (((1,), (1,)), ((), ())), preferred_element_type=F32)
            if diagonal:
                s = jnp.where(causal, s, -jnp.inf)
            m_new = jnp.maximum(m, jnp.max(s, axis=-1, keepdims=True))
            alpha = jnp.exp(m - m_new)
            p = jnp.exp(s - m_new)
            l_new = alpha * l + jnp.sum(p, axis=-1, keepdims=True)
            acc_new = alpha * acc + jnp.dot(p.astype(BF16), vs, preferred_element_type=F32)
            out.append((m_new, l_new, acc_new))
        return tuple(out)

    init = tuple((jnp.full((tq, 1), NEG_BIG, F32), jnp.zeros((tq, 1), F32),
                  jnp.zeros((tq, LANES), F32)) for _ in range(n_maps))
    carry = lax.fori_loop(0, i, functools.partial(step, diagonal=False), init)
    carry = step(i, carry, diagonal=True)

    lp = lam_ref[...]
    lam = (jnp.exp(jnp.sum(lp[0:1] * lp[1:2], axis=-1, keepdims=True))
           - jnp.exp(jnp.sum(lp[2:3] * lp[3:4], axis=-1, keepdims=True)) + lambda_init)
    heads = []
    for hh in range(HEADS_PER_VREG):
        (_, l0, a0), (_, l1, a1) = carry[2 * hh], carry[2 * hh + 1]
        heads.append(a0 * (1.0 / l0) - lam * (a1 * (1.0 / l1)))
    first = lane < HEAD_DIM
    o = jnp.where(first, heads[0], heads[1])
    o2 = o * o
    ss0 = jnp.sum(jnp.where(first, o2, 0.0), axis=-1, keepdims=True)
    ss1 = jnp.sum(jnp.where(first, 0.0, o2), axis=-1, keepdims=True)
    ms = jnp.where(first, ss0, ss1) * (1.0 / HEAD_DIM)
    o = o * lax.rsqrt(ms + NORM_EPS)
    o_ref[0] = (o * sg_ref[...] * (1.0 - lambda_init)).astype(o_ref.dtype)


def _attention(att3, lam_params, subln_pair, lambda_init):
    b, s, _ = att3.shape
    tq = ATT_TILE
    n_pairs = ATT_W // LANES
    return pl.pallas_call(
        functools.partial(_attn_kernel, lambda_init=lambda_init),
        grid=(b, n_pairs, s // tq),
        in_specs=[_const_spec(lam_params.shape), _const_spec(subln_pair.shape),
                  pl.BlockSpec((1, tq, LANES), lambda bb, hp, i: (bb, i, hp)),
                  pl.BlockSpec((1, s, LANES), lambda bb, hp, i: (bb, 0, n_pairs + hp)),
                  pl.BlockSpec((1, s, LANES), lambda bb, hp, i: (bb, 0, 2 * n_pairs + hp))],
        out_specs=pl.BlockSpec((1, tq, LANES), lambda bb, hp, i: (bb, i, hp)),
        out_shape=jax.ShapeDtypeStruct((b, s, ATT_W), BF16),
        compiler_params=_compiler_params(3),
        name="diff_attn",
    )(lam_params, subln_pair, att3, att3, att3)


def _tri_inverse(a, r_idx, c_idx):
    n = a.shape[0]
    base = 16
    same = lambda blk: (r_idx // blk) == (c_idx // blk)
    eye = (r_idx == c_idx).astype(F32)
    d = jnp.where(same(base), a, 0.0)
    inv = eye + d
    power = d
    span = 2
    while span < base:
        power = _mm(power, power)
        inv = inv + _mm(power, inv)
        span *= 2
    blk = base
    while blk < n:
        e = jnp.where(jnp.logical_and(same(2 * blk), jnp.logical_not(same(blk))), a, 0.0)
        inv = inv + _mm(inv, _mm(e, inv))
        blk *= 2
    return inv


def _rwkv_kernel(p_ref, mu_ref, vec_ref, wup_ref, aup_ref, gup_ref, tril_ref, hsum_ref,
                 o_ref, state_ref, prev_ref):
    c = p_ref.shape[1]
    w = RWKV_W

    @pl.when(pl.program_id(1) == 0)
    def _():
        state_ref[...] = jnp.zeros_like(state_ref)
        prev_ref[...] = jnp.zeros_like(prev_ref)

    p = p_ref[0]
    row1 = lax.broadcasted_iota(jnp.int32, (c, 1), 0)
    prev = jnp.where(row1 == 0, prev_ref[SUBLANES - 1:SUBLANES, :], pltpu.roll(p, 1, 0))
    prev_ref[...] = p[c - SUBLANES:, :]
    z = p + mu_ref[...] * (prev - p)

    vec = vec_ref[...]
    w0, a0, k_k, k_a, r_k, ln_g, ln_b = (vec[n:n + 1] for n in range(7))
    r = z[:, 0:w]
    k = z[:, w:2 * w]
    v = z[:, 2 * w:3 * w]
    lora_in = z[:, 3 * w:3 * w + DECAY_LORA + ICLR_LORA]
    g_in = z[:, 3 * w + DECAY_LORA + ICLR_LORA:]

    wdec = w0 + _mm(jnp.tanh(lora_in), wup_ref[...])
    logw = -math.exp(-0.5) * jax.nn.sigmoid(wdec)
    iclr = jax.nn.sigmoid(a0 + _mm(lora_in, aup_ref[...]))
    gate = _mm(jax.nn.sigmoid(g_in), gup_ref[...])

    hsum = hsum_ref[...]
    kk = k * k_k
    kk_norm = jnp.maximum(jnp.sqrt(_mm_split(kk * kk, hsum, 2)), 1e-12)
    kk = kk / kk_norm
    k = k * (1.0 + (iclr - 1.0) * k_a)
    a_vec = -kk
    b_vec = kk * iclr

    lcum = _cumsum_rows(tril_ref[...], logw)
    lmid = lcum[c // 2 - 1:c // 2, :]
    lend = lcum[c - 1:c, :]
    lc = lcum - lmid
    g_incl = jnp.exp(lc)
    g_excl = jnp.exp(lc - logw)
    g_inv = jnp.exp(-lc)
    g_tail = jnp.exp(lend - lcum)
    g_mid = jnp.exp(lmid)
    g_end = jnp.exp(lend)

    a_til = a_vec * g_excl
    r_til = r * g_incl
    b_chk = b_vec * g_inv
    k_chk = k * g_inv
    b_hat = b_vec * g_tail
    k_hat = k * g_tail

    r_idx = lax.broadcasted_iota(jnp.int32, (c, c), 0)
    c_idx = lax.broadcasted_iota(jnp.int32, (c, c), 1)
    strict = c_idx < r_idx
    incl = c_idx <= r_idx
    strict2 = jnp.concatenate([strict, strict], axis=1)
    incl2 = jnp.concatenate([incl, incl], axis=1)
    lane_head = lax.broadcasted_iota(jnp.int32, (2 * c, LANES), 1) // HEAD_DIM
    lane_head_c = lax.broadcasted_iota(jnp.int32, (c, LANES), 1) // HEAD_DIM
    blk_r = lax.broadcasted_iota(jnp.int32, (LANES, LANES), 0) // HEAD_DIM
    blk_c = lax.broadcasted_iota(jnp.int32, (LANES, LANES), 1) // HEAD_DIM
    same_head = blk_r == blk_c

    ys = []
    for pr in range(w // LANES):
        sl = slice(pr * LANES, (pr + 1) * LANES)
        ar = jnp.concatenate([a_til[:, sl], r_til[:, sl]], axis=0).astype(BF16)
        bk_chk = jnp.concatenate([b_chk[:, sl], k_chk[:, sl]], axis=0).astype(BF16)
        bk_hat = jnp.concatenate([b_hat[:, sl], k_hat[:, sl]], axis=0).astype(BF16)
        vp = v[:, sl]
        state = state_ref[pr]
        x = _mm_nt(ar, state * g_mid[:, sl])
        xa, xr = x[:c], x[c:]
        u_pair = None
        w_rs = []
        for hh in range(HEADS_PER_VREG):
            g = _mm_nt(jnp.where(lane_head == hh, ar, jnp.zeros_like(ar)), bk_chk)
            w_a = jnp.where(strict2, g[:c], 0.0)
            w_rs.append(jnp.where(incl2, g[c:], 0.0))
            t_inv = _tri_inverse(w_a[:, :c], r_idx, c_idx)
            u_h = _mm(t_inv, xa + _mm(w_a[:, c:], vp))
            u_pair = u_h if hh == 0 else jnp.where(lane_head_c == 0, u_pair, u_h)
        uv = jnp.concatenate([u_pair, vp], axis=0)
        y_pair = None
        for hh in range(HEADS_PER_VREG):
            y_h = xr + _mm(w_rs[hh], uv)
            y_pair = y_h if hh == 0 else jnp.where(lane_head_c == 0, y_pair, y_h)
        ys.append(y_pair)
        state_ref[pr] = state * g_end[:, sl] + jnp.where(same_head, _mm_tn(uv, bk_hat), 0.0)

    y = jnp.concatenate(ys, axis=1)
    mean = _mm_split(y, hsum, 2) * (1.0 / HEAD_DIM)
    yc = y - mean
    var = _mm_split(yc * yc, hsum, 2) * (1.0 / HEAD_DIM)
    yn = yc * lax.rsqrt(var + GN_EPS) * ln_g + ln_b
    bonus = _mm_split(r * k * r_k, hsum, 2) * v
    o_ref[0] = ((yn + bonus) * gate).astype(o_ref.dtype)


def _rwkv(rw3, mu, vec, wup, aup, gup, tril, hsum):
    b, s, cols = rw3.shape
    c = RWKV_CHUNK
    return pl.pallas_call(
        _rwkv_kernel,
        grid=(b, s // c),
        in_specs=[pl.BlockSpec((1, c, cols), lambda bb, i: (bb, i, 0)),
                  _const_spec(mu.shape), _const_spec(vec.shape), _const_spec(wup.shape),
                  _const_spec(aup.shape), _const_spec(gup.shape), _const_spec(tril.shape),
                  _const_spec(hsum.shape)],
        out_specs=pl.BlockSpec((1, c, RWKV_W), lambda bb, i: (bb, i, 0)),
        out_shape=jax.ShapeDtypeStruct((b, s, RWKV_W), BF16),
        scratch_shapes=[pltpu.VMEM((RWKV_W // LANES, LANES, LANES), F32),
                        pltpu.VMEM((SUBLANES, cols), F32)],
        compiler_params=_compiler_params(2),
        name="rwkv7",
    )(rw3, mu, vec, wup, aup, gup, tril, hsum)


def _mlp_kernel(x_ref, att_ref, conv_ref, halo_ref, rw_ref, convw_ref, wo_att_ref, wo_conv_ref,
                wo_rw_ref, g_ref, wup_ref, wdn_ref, gfin_ref, o_ref, *, tiles_per_seq, final):
    tm = x_ref.shape[0]
    cw = CONV_W
    conv = conv_ref[...]
    zc = conv[:, cw:2 * cw] * conv[:, 2 * cw:]
    halo = halo_ref[...]
    seq_start = (pl.program_id(0) % tiles_per_seq) == 0
    zh = jnp.where(seq_start, 0.0, halo[:, cw:2 * cw] * halo[:, 2 * cw:])
    row = lax.broadcasted_iota(jnp.int32, (tm, 1), 0)
    z1 = jnp.where(row == 0, zh[SUBLANES - 1:SUBLANES], pltpu.roll(zc, 1, 0))
    z2 = jnp.where(row == 0, zh[SUBLANES - 2:SUBLANES - 1],
                   jnp.where(row == 1, zh[SUBLANES - 1:SUBLANES], pltpu.roll(zc, 2, 0)))
    cwt = convw_ref[...]
    o_conv = conv[:, :cw] * (cwt[0:1] * z2 + cwt[1:2] * z1 + cwt[2:3] * zc)

    x = x_ref[...]
    x = x + (jnp.dot(att_ref[...], wo_att_ref[...], preferred_element_type=F32)
             + jnp.dot(o_conv.astype(BF16), wo_conv_ref[...], preferred_element_type=F32)
             + jnp.dot(rw_ref[...], wo_rw_ref[...], preferred_element_type=F32))

    h = _rms(x, g_ref[...]).astype(BF16)
    d_ff = wup_ref.shape[1]
    acc = x
    for j in range(d_ff // FF_CHUNK):
        cols = slice(j * FF_CHUNK, (j + 1) * FF_CHUNK)
        up = jnp.maximum(jnp.dot(h, wup_ref[:, cols], preferred_element_type=F32), 0.0)
        acc = acc + jnp.dot((up * up).astype(BF16), wdn_ref[cols, :], preferred_element_type=F32)
    if final:
        acc = _rms(acc, gfin_ref[...])
    o_ref[...] = acc


def _mlp(x2, att2, conv2, rw2, convw, wo_att, wo_conv, wo_rw, g, wup, wdn, gfin, *, seq, final):
    t, d = x2.shape
    tm = ROW_TILE
    halo_blocks = tm // SUBLANES
    row = lambda n: pl.BlockSpec((tm, n), lambda i: (i, 0))
    halo = pl.BlockSpec((SUBLANES, conv2.shape[1]),
                        lambda i: (jnp.maximum(i * halo_blocks - 1, 0), 0))
    consts = [convw, wo_att, wo_conv, wo_rw, g, wup, wdn, gfin]
    return pl.pallas_call(
        functools.partial(_mlp_kernel, tiles_per_seq=seq // tm, final=final),
        grid=(t // tm,),
        in_specs=[row(d), row(att2.shape[1]), row(conv2.shape[1]), halo, row(rw2.shape[1])]
                 + [_const_spec(a.shape) for a in consts],
        out_specs=row(d),
        out_shape=jax.ShapeDtypeStruct((t, d), F32),
        compiler_params=_compiler_params(1),
        name="mix_mlp",
    )(x2, att2, conv2, conv2, rw2, *consts)


def kernel(x, norm_mix_g, w_in, lam_q1, lam_k1, lam_q2, lam_k2, subln_g, conv_w, shift_mu, rwkv_w0, rwkv_w_up, rwkv_a0, rwkv_a_up, rwkv_g_up, rwkv_k_k, rwkv_k_a, rwkv_r_k, lnx_g, lnx_b, w_out, norm_mlp_g, w_mlp_up, w_mlp_down, final_norm_g):
    bsz, seq, d = x.shape
    depth = w_in.shape[0]
    t = bsz * seq
    assert t % ROW_TILE == 0 and seq % ROW_TILE == 0 and seq % ATT_TILE == 0 and seq % RWKV_CHUNK == 0

    att_cols, conv_cols = 3 * ATT_W, 3 * CONV_W
    qscale = jnp.concatenate([jnp.full((1, ATT_W), DIFF_HALF ** -0.5, F32),
                              jnp.ones((1, 2 * ATT_W), F32)], axis=1)
    idx = jnp.arange(RWKV_CHUNK)
    tril = (idx[None, :] <= idx[:, None]).astype(BF16)
    hidx = jnp.arange(RWKV_W) // HEAD_DIM
    hsum = (hidx[:, None] == hidx[None, :]).astype(BF16)
    lora_pad = jnp.zeros((DECAY_LORA, RWKV_W), F32)

    x2 = x.reshape(t, d)
    for l in range(depth):
        lambda_init = 0.8 - 0.6 * math.exp(-0.3 * l)
        w_l = w_in[l].astype(BF16)
        att2, conv2, rw2 = _inproj(
            x2, norm_mix_g[l][None], qscale, w_l[:, :att_cols],
            w_l[:, att_cols:att_cols + conv_cols], w_l[:, att_cols + conv_cols:])

        lam_params = jnp.stack([lam_q1[l], lam_k1[l], lam_q2[l], lam_k2[l]])
        subln_pair = jnp.tile(subln_g[l], HEADS_PER_VREG)[None]
        o_att = _attention(att2.reshape(bsz, seq, att_cols), lam_params, subln_pair, lambda_init)

        vec = jnp.stack([rwkv_w0[l], rwkv_a0[l], rwkv_k_k[l], rwkv_k_a[l], rwkv_r_k[l].reshape(-1),
                         lnx_g[l], lnx_b[l], jnp.zeros((RWKV_W,), F32)])
        wup = jnp.concatenate([rwkv_w_up[l], lora_pad], axis=0).astype(BF16)
        aup = jnp.concatenate([lora_pad, rwkv_a_up[l]], axis=0).astype(BF16)
        o_rw = _rwkv(rw2.reshape(bsz, seq, RWKV_COLS), shift_mu[l][None], vec, wup, aup,
                     rwkv_g_up[l].astype(BF16), tril, hsum)

        wo = w_out[l].astype(BF16)
        x2 = _mlp(x2, o_att.reshape(t, ATT_W), conv2, o_rw.reshape(t, RWKV_W), conv_w[l],
                  wo[:ATT_W], wo[ATT_W:ATT_W + CONV_W], wo[ATT_W + CONV_W:], norm_mlp_g[l][None],
                  w_mlp_up[l].astype(BF16), w_mlp_down[l].astype(BF16), final_norm_g[None],
                  seq=seq, final=(l == depth - 1))
    return x2.reshape(bsz, seq, d)
```

```python
import functools
import math

import jax
import jax.numpy as jnp
from jax import lax
from jax.experimental import pallas as pl
from jax.experimental.pallas import tpu as pltpu

F32 = jnp.float32
BF16 = jnp.bfloat16

HEAD_DIM = 64
DIFF_HALF = HEAD_DIM // 2
ATT_HEADS = 6
CONV_GROUPS = 4
RWKV_HEADS = 6
ATT_W = ATT_HEADS * HEAD_DIM
CONV_W = CONV_GROUPS * HEAD_DIM
RWKV_W = RWKV_HEADS * HEAD_DIM
CONV_K = 3
DECAY_LORA = 64
ICLR_LORA = 64
GATE_LORA = 128
RWKV_COLS = 3 * RWKV_W + DECAY_LORA + ICLR_LORA + GATE_LORA
NORM_EPS = 1e-6
GN_EPS = 64e-5

LANES = 128
SUBLANES = 8
HEADS_PER_VREG = LANES // HEAD_DIM
VMEM_LIMIT_BYTES = 56 * 1024 * 1024

ROW_TILE = 512
ATT_TILE = 512
RWKV_CHUNK = 128
FF_CHUNK = 1024
NEG_BIG = -1e30


def _compiler_params(n_axes):
    return pltpu.CompilerParams(
        dimension_semantics=("arbitrary",) * n_axes,
        vmem_limit_bytes=VMEM_LIMIT_BYTES,
    )


def _const_spec(shape):
    nd = len(shape)
    return pl.BlockSpec(shape, lambda *_: (0,) * nd)


def _mm(a, b):
    return jnp.dot(a.astype(BF16), b.astype(BF16), preferred_element_type=F32)


def _mm_nt(a, b):
    return lax.dot_general(a.astype(BF16), b.astype(BF16), (((1,), (1,)), ((), ())),
                           preferred_element_type=F32)


def _mm_tn(a, b):
    return lax.dot_general(a.astype(BF16), b.astype(BF16), (((0,), (0,)), ((), ())),
                           preferred_element_type=F32)


def _mm_split(x, ones_bf16, terms):
    acc = None
    rem = x
    for t in range(terms):
        piece = rem.astype(BF16)
        d = jnp.dot(piece, ones_bf16, preferred_element_type=F32)
        acc = d if acc is None else acc + d
        if t + 1 < terms:
            rem = rem - piece.astype(F32)
    return acc


def _cumsum_rows(tril_bf16, x):
    acc = None
    rem = x
    for t in range(3):
        piece = rem.astype(BF16)
        d = jnp.dot(tril_bf16, piece, preferred_element_type=F32)
        acc = d if acc is None else acc + d
        if t < 2:
            rem = rem - piece.astype(F32)
    return acc


def _rms(x, g):
    ms = jnp.mean(x * x, axis=-1, keepdims=True)
    return x * lax.rsqrt(ms + NORM_EPS) * g


def _inproj_kernel(x_ref, g_ref, qscale_ref, w_qk_ref, w_vt_ref, w_conv_ref, w_rw_ref,
                   qk_ref, vt_ref, conv_ref, rw_ref):
    h = _rms(x_ref[...], g_ref[...]).astype(BF16)
    qk = jnp.dot(h, w_qk_ref[...], preferred_element_type=F32)
    qk_ref[...] = (qk * qscale_ref[...]).astype(BF16)
    vt_ref[...] = _mm_nt(w_vt_ref[...], h).astype(BF16)
    conv_ref[...] = jnp.dot(h, w_conv_ref[...], preferred_element_type=F32)
    rw_ref[...] = jnp.dot(h, w_rw_ref[...], preferred_element_type=F32)


def _inproj(x2, g, qscale, w_qk, w_vt, w_conv, w_rw):
    t, d = x2.shape
    tm = ROW_TILE
    row = lambda n: pl.BlockSpec((tm, n), lambda i: (i, 0))
    return pl.pallas_call(
        _inproj_kernel,
        grid=(t // tm,),
        in_specs=[row(d), _const_spec(g.shape), _const_spec(qscale.shape), _const_spec(w_qk.shape),
                  _const_spec(w_vt.shape), _const_spec(w_conv.shape), _const_spec(w_rw.shape)],
        out_specs=[row(2 * ATT_W), pl.BlockSpec((ATT_W, tm), lambda i: (0, i)),
                   row(3 * CONV_W), row(RWKV_COLS)],
        out_shape=[jax.ShapeDtypeStruct((t, 2 * ATT_W), BF16),
                   jax.ShapeDtypeStruct((ATT_W, t), BF16),
                   jax.ShapeDtypeStruct((t, 3 * CONV_W), F32),
                   jax.ShapeDtypeStruct((t, RWKV_COLS), F32)],
        compiler_params=_compiler_params(1),
        name="inproj",
    )(x2, g, qscale, w_qk, w_vt, w_conv, w_rw)


def _attn_kernel(lam_ref, sg_ref, q_ref, k_ref, vt_ref, o_ref, *, lambda_init):
    tq = q_ref.shape[1]
    i = pl.program_id(2)
    q = q_ref[0]
    lane = lax.broadcasted_iota(jnp.int32, (tq, LANES), 1)
    n_maps = LANES // DIFF_HALF
    q_sel = [jnp.where(lane // DIFF_HALF == c, q, jnp.zeros_like(q)) for c in range(n_maps)]
    key_idx = lax.broadcasted_iota(jnp.int32, (tq, tq), 0)
    qry_idx = lax.broadcasted_iota(jnp.int32, (tq, tq), 1)
    causal = key_idx <= qry_idx

    def step(j, carry, diagonal):
        start = pl.multiple_of(j * tq, tq)
        ks = k_ref[0, pl.ds(start, tq), :]
        vt = vt_ref[:, pl.ds(start, tq)]
        scores = [_mm_nt(ks, q_sel[c]) for c in range(n_maps)]
        stats = []
        for c in range(n_maps):
            m, l, _ = carry[c]
            s = jnp.where(causal, scores[c], -jnp.inf) if diagonal else scores[c]
            m_new = jnp.maximum(m, jnp.max(s, axis=0, keepdims=True))
            alpha = jnp.exp2(m - m_new)
            p = jnp.exp2(s - m_new)
            l_new = alpha * l + jnp.sum(p, axis=0, keepdims=True)
            stats.append((m_new, l_new, alpha, p.astype(BF16)))
        out = []
        for c, (m_new, l_new, alpha, p) in enumerate(stats):
            hh = c // 2
            pv = jnp.dot(vt[hh * HEAD_DIM:(hh + 1) * HEAD_DIM], p, preferred_element_type=F32)
            out.append((m_new, l_new, alpha * carry[c][2] + pv))
        return tuple(out)

    init = tuple((jnp.full((1, tq), NEG_BIG, F32), jnp.zeros((1, tq), F32),
                  jnp.zeros((HEAD_DIM, tq), F32)) for _ in range(n_maps))
    carry = lax.fori_loop(0, i, functools.partial(step, diagonal=False), init)
    carry = step(i, carry, diagonal=True)

    lp = lam_ref[...]
    lam = (jnp.exp(jnp.sum(lp[0:1] * lp[1:2], axis=-1, keepdims=True))
           - jnp.exp(jnp.sum(lp[2:3] * lp[3:4], axis=-1, keepdims=True)) + lambda_init)
    heads = []
    for hh in range(HEADS_PER_VREG):
        (_, l0, a0), (_, l1, a1) = carry[2 * hh], carry[2 * hh + 1]
        o = a0 * (1.0 / l0) - lam * (a1 * (1.0 / l1))
        ms = jnp.mean(o * o, axis=0, keepdims=True)
        heads.append(o * lax.rsqrt(ms + NORM_EPS) * sg_ref[...] * (1.0 - lambda_init))
    o_ref[0] = jnp.concatenate(heads, axis=0).T.astype(o_ref.dtype)


def _attention(qk3, vt, lam_params, subln_col, lambda_init):
    b, s, _ = qk3.shape
    tq = ATT_TILE
    n_pairs = ATT_W // LANES
    return pl.pallas_call(
        functools.partial(_attn_kernel, lambda_init=lambda_init),
        grid=(b, n_pairs, s // tq),
        in_specs=[_const_spec(lam_params.shape), _const_spec(subln_col.shape),
                  pl.BlockSpec((1, tq, LANES), lambda bb, hp, i: (bb, i, hp)),
                  pl.BlockSpec((1, s, LANES), lambda bb, hp, i: (bb, 0, n_pairs + hp)),
                  pl.BlockSpec((LANES, s), lambda bb, hp, i: (hp, bb))],
        out_specs=pl.BlockSpec((1, tq, LANES), lambda bb, hp, i: (bb, i, hp)),
        out_shape=jax.ShapeDtypeStruct((b, s, ATT_W), BF16),
        compiler_params=_compiler_params(3),
        name="diff_attn",
    )(lam_params, subln_col, qk3, qk3, vt)


def _tri_inverse_all(mats, r_idx, c_idx):
    n = mats[0].shape[0]
    base = 16
    same = lambda blk: (r_idx // blk) == (c_idx // blk)
    eye = (r_idx == c_idx).astype(F32)
    in_base = same(base)
    powers = [jnp.where(in_base, a, 0.0) for a in mats]
    invs = [eye + d for d in powers]
    span = 2
    while span < base:
        powers = [_mm(pw, pw) for pw in powers]
        invs = [inv + _mm(pw, inv) for pw, inv in zip(powers, invs)]
        span *= 2
    blk = base
    while blk < n:
        off_diag = jnp.logical_and(same(2 * blk), jnp.logical_not(same(blk)))
        tmp = [_mm(jnp.where(off_diag, a, 0.0), inv) for a, inv in zip(mats, invs)]
        invs = [inv + _mm(inv, t) for inv, t in zip(invs, tmp)]
        blk *= 2
    return invs


def _rwkv_kernel(p_ref, mu_ref, vec_ref, wup_ref, aup_ref, gup_ref, tril_ref, hsum_ref,
                 o_ref, state_ref, prev_ref):
    c = p_ref.shape[1]
    w = RWKV_W

    @pl.when(pl.program_id(1) == 0)
    def _():
        state_ref[...] = jnp.zeros_like(state_ref)
        prev_ref[...] = jnp.zeros_like(prev_ref)

    p = p_ref[0]
    row1 = lax.broadcasted_iota(jnp.int32, (c, 1), 0)
    prev = jnp.where(row1 == 0, prev_ref[SUBLANES - 1:SUBLANES, :], pltpu.roll(p, 1, 0))
    prev_ref[...] = p[c - SUBLANES:, :]
    z = p + mu_ref[...] * (prev - p)

    vec = vec_ref[...]
    w0, a0, k_k, k_a, r_k, ln_g, ln_b = (vec[n:n + 1] for n in range(7))
    r = z[:, 0:w]
    k = z[:, w:2 * w]
    v = z[:, 2 * w:3 * w]
    lora_in = z[:, 3 * w:3 * w + DECAY_LORA + ICLR_LORA]
    g_in = z[:, 3 * w + DECAY_LORA + ICLR_LORA:]

    wdec = w0 + _mm(jnp.tanh(lora_in), wup_ref[...])
    logw = -math.exp(-0.5) * jax.nn.sigmoid(wdec)
    iclr = jax.nn.sigmoid(a0 + _mm(lora_in, aup_ref[...]))
    gate = _mm(jax.nn.sigmoid(g_in), gup_ref[...])

    hsum = hsum_ref[...]
    kk = k * k_k
    kk_norm = jnp.maximum(jnp.sqrt(_mm_split(kk * kk, hsum, 2)), 1e-12)
    kk = kk / kk_norm
    k = k * (1.0 + (iclr - 1.0) * k_a)
    a_vec = -kk
    b_vec = kk * iclr

    lcum = _cumsum_rows(tril_ref[...], logw)
    lmid = lcum[c // 2 - 1:c // 2, :]
    lend = lcum[c - 1:c, :]
    lc = lcum - lmid
    g_incl = jnp.exp(lc)
    g_excl = jnp.exp(lc - logw)
    g_inv = jnp.exp(-lc)
    g_tail = jnp.exp(lend - lcum)
    g_mid = jnp.exp(lmid)
    g_end = jnp.exp(lend)

    a_til = a_vec * g_excl
    r_til = r * g_incl
    b_chk = b_vec * g_inv
    k_chk = k * g_inv
    b_hat = b_vec * g_tail
    k_hat = k * g_tail

    r_idx = lax.broadcasted_iota(jnp.int32, (c, c), 0)
    c_idx = lax.broadcasted_iota(jnp.int32, (c, c), 1)
    strict = c_idx < r_idx
    incl = c_idx <= r_idx
    strict2 = jnp.concatenate([strict, strict], axis=1)
    incl2 = jnp.concatenate([incl, incl], axis=1)
    lane_head = lax.broadcasted_iota(jnp.int32, (2 * c, LANES), 1) // HEAD_DIM
    lane_head_c = lax.broadcasted_iota(jnp.int32, (c, LANES), 1) // HEAD_DIM
    blk_r = lax.broadcasted_iota(jnp.int32, (LANES, LANES), 0) // HEAD_DIM
    blk_c = lax.broadcasted_iota(jnp.int32, (LANES, LANES), 1) // HEAD_DIM
    same_head = blk_r == blk_c

    pairs = range(w // LANES)
    heads = [(pr, hh) for pr in pairs for hh in range(HEADS_PER_VREG)]
    sls = [slice(pr * LANES, (pr + 1) * LANES) for pr in pairs]
    ar = [jnp.concatenate([a_til[:, sl], r_til[:, sl]], axis=0).astype(BF16) for sl in sls]
    bk_chk = [jnp.concatenate([b_chk[:, sl], k_chk[:, sl]], axis=0).astype(BF16) for sl in sls]
    bk_hat = [jnp.concatenate([b_hat[:, sl], k_hat[:, sl]], axis=0).astype(BF16) for sl in sls]
    vp = [v[:, sl] for sl in sls]
    vp_bf = [t.astype(BF16) for t in vp]
    state = [state_ref[pr] for pr in pairs]

    g = [_mm_nt(jnp.where(lane_head == hh, ar[pr], jnp.zeros_like(ar[pr])), bk_chk[pr])
         for pr, hh in heads]
    x = [_mm_nt(ar[pr], state[pr] * g_mid[:, sls[pr]]) for pr in pairs]
    w_a = [jnp.where(strict2, gh[:c], 0.0) for gh in g]
    w_r = [jnp.where(incl2, gh[c:], 0.0) for gh in g]
    ak_v = [_mm(w_a[n][:, c:], vp_bf[pr]) for n, (pr, hh) in enumerate(heads)]
    t_inv = _tri_inverse_all([wa[:, :c] for wa in w_a], r_idx, c_idx)
    u_h = [_mm(t_inv[n], x[pr][:c] + ak_v[n]) for n, (pr, hh) in enumerate(heads)]
    uv = [jnp.concatenate([jnp.where(lane_head_c == 0, u_h[2 * pr], u_h[2 * pr + 1]), vp[pr]],
                          axis=0).astype(BF16) for pr in pairs]
    y_h = [x[pr][c:] + _mm(w_r[n], uv[pr]) for n, (pr, hh) in enumerate(heads)]
    for pr in pairs:
        state_ref[pr] = (state[pr] * g_end[:, sls[pr]]
                         + jnp.where(same_head, _mm_tn(uv[pr], bk_hat[pr]), 0.0))
    y = jnp.concatenate([jnp.where(lane_head_c == 0, y_h[2 * pr], y_h[2 * pr + 1]) for pr in pairs],
                        axis=1)
    mean = _mm_split(y, hsum, 2) * (1.0 / HEAD_DIM)
    yc = y - mean
    var = _mm_split(yc * yc, hsum, 2) * (1.0 / HEAD_DIM)
    yn = yc * lax.rsqrt(var + GN_EPS) * ln_g + ln_b
    bonus = _mm_split(r * k * r_k, hsum, 2) * v
    o_ref[0] = ((yn + bonus) * gate).astype(o_ref.dtype)


def _rwkv(rw3, mu, vec, wup, aup, gup, tril, hsum):
    b, s, cols = rw3.shape
    c = RWKV_CHUNK
    return pl.pallas_call(
        _rwkv_kernel,
        grid=(b, s // c),
        in_specs=[pl.BlockSpec((1, c, cols), lambda bb, i: (bb, i, 0)),
                  _const_spec(mu.shape), _const_spec(vec.shape), _const_spec(wup.shape),
                  _const_spec(aup.shape), _const_spec(gup.shape), _const_spec(tril.shape),
                  _const_spec(hsum.shape)],
        out_specs=pl.BlockSpec((1, c, RWKV_W), lambda bb, i: (bb, i, 0)),
        out_shape=jax.ShapeDtypeStruct((b, s, RWKV_W), BF16),
        scratch_shapes=[pltpu.VMEM((RWKV_W // LANES, LANES, LANES), F32),
                        pltpu.VMEM((SUBLANES, cols), F32)],
        compiler_params=_compiler_params(2),
        name="rwkv7",
    )(rw3, mu, vec, wup, aup, gup, tril, hsum)


def _mlp_kernel(x_ref, att_ref, conv_ref, halo_ref, rw_ref, convw_ref, wo_att_ref, wo_conv_ref,
                wo_rw_ref, g_ref, wup_ref, wdn_ref, gfin_ref, o_ref, *, tiles_per_seq, final):
    tm = x_ref.shape[0]
    cw = CONV_W
    conv = conv_ref[...]
    zc = conv[:, cw:2 * cw] * conv[:, 2 * cw:]
    halo = halo_ref[...]
    seq_start = (pl.program_id(0) % tiles_per_seq) == 0
    zh = jnp.where(seq_start, 0.0, halo[:, cw:2 * cw] * halo[:, 2 * cw:])
    row = lax.broadcasted_iota(jnp.int32, (tm, 1), 0)
    z1 = jnp.where(row == 0, zh[SUBLANES - 1:SUBLANES], pltpu.roll(zc, 1, 0))
    z2 = jnp.where(row == 0, zh[SUBLANES - 2:SUBLANES - 1],
                   jnp.where(row == 1, zh[SUBLANES - 1:SUBLANES], pltpu.roll(zc, 2, 0)))
    cwt = convw_ref[...]
    o_conv = conv[:, :cw] * (cwt[0:1] * z2 + cwt[1:2] * z1 + cwt[2:3] * zc)

    x = x_ref[...]
    x = x + (jnp.dot(att_ref[...], wo_att_ref[...], preferred_element_type=F32)
             + jnp.dot(o_conv.astype(BF16), wo_conv_ref[...], preferred_element_type=F32)
             + jnp.dot(rw_ref[...], wo_rw_ref[...], preferred_element_type=F32))

    h = _rms(x, g_ref[...]).astype(BF16)
    d_ff = wup_ref.shape[1]
    acc = x
    for j in range(d_ff // FF_CHUNK):
        cols = slice(j * FF_CHUNK, (j + 1) * FF_CHUNK)
        up = jnp.maximum(jnp.dot(h, wup_ref[:, cols], preferred_element_type=F32), 0.0)
        acc = acc + jnp.dot((up * up).astype(BF16), wdn_ref[cols, :], preferred_element_type=F32)
    if final:
        acc = _rms(acc, gfin_ref[...])
    o_ref[...] = acc


def _mlp(x2, att2, conv2, rw2, convw, wo_att, wo_conv, wo_rw, g, wup, wdn, gfin, *, seq, final):
    t, d = x2.shape
    tm = ROW_TILE
    halo_blocks = tm // SUBLANES
    row = lambda n: pl.BlockSpec((tm, n), lambda i: (i, 0))
    halo = pl.BlockSpec((SUBLANES, conv2.shape[1]),
                        lambda i: (jnp.maximum(i * halo_blocks - 1, 0), 0))
    consts = [convw, wo_att, wo_conv, wo_rw, g, wup, wdn, gfin]
    return pl.pallas_call(
        functools.partial(_mlp_kernel, tiles_per_seq=seq // tm, final=final),
        grid=(t // tm,),
        in_specs=[row(d), row(att2.shape[1]), row(conv2.shape[1]), halo, row(rw2.shape[1])]
                 + [_const_spec(a.shape) for a in consts],
        out_specs=row(d),
        out_shape=jax.ShapeDtypeStruct((t, d), F32),
        compiler_params=_compiler_params(1),
        name="mix_mlp",
    )(x2, att2, conv2, conv2, rw2, *consts)


def kernel(x, norm_mix_g, w_in, lam_q1, lam_k1, lam_q2, lam_k2, subln_g, conv_w, shift_mu, rwkv_w0, rwkv_w_up, rwkv_a0, rwkv_a_up, rwkv_g_up, rwkv_k_k, rwkv_k_a, rwkv_r_k, lnx_g, lnx_b, w_out, norm_mlp_g, w_mlp_up, w_mlp_down, final_norm_g):
    bsz, seq, d = x.shape
    depth = w_in.shape[0]
    t = bsz * seq
    assert t % ROW_TILE == 0 and seq % ROW_TILE == 0 and seq % ATT_TILE == 0 and seq % RWKV_CHUNK == 0

    att_cols, conv_cols = 3 * ATT_W, 3 * CONV_W
    qscale = jnp.concatenate([jnp.full((1, ATT_W), DIFF_HALF ** -0.5 * math.log2(math.e), F32),
                              jnp.ones((1, ATT_W), F32)], axis=1)
    idx = jnp.arange(RWKV_CHUNK)
    tril = (idx[None, :] <= idx[:, None]).astype(BF16)
    hidx = jnp.arange(RWKV_W) // HEAD_DIM
    hsum = (hidx[:, None] == hidx[None, :]).astype(BF16)
    lora_pad = jnp.zeros((DECAY_LORA, RWKV_W), F32)

    x2 = x.reshape(t, d)
    for l in range(depth):
        lambda_init = 0.8 - 0.6 * math.exp(-0.3 * l)
        w_l = w_in[l].astype(BF16)
        qk2, vt, conv2, rw2 = _inproj(
            x2, norm_mix_g[l][None], qscale, w_l[:, :2 * ATT_W], w_l[:, 2 * ATT_W:att_cols].T,
            w_l[:, att_cols:att_cols + conv_cols], w_l[:, att_cols + conv_cols:])

        lam_params = jnp.stack([lam_q1[l], lam_k1[l], lam_q2[l], lam_k2[l]])
        o_att = _attention(qk2.reshape(bsz, seq, 2 * ATT_W), vt, lam_params, subln_g[l][:, None],
                           lambda_init)

        vec = jnp.stack([rwkv_w0[l], rwkv_a0[l], rwkv_k_k[l], rwkv_k_a[l], rwkv_r_k[l].reshape(-1),
                         lnx_g[l], lnx_b[l], jnp.zeros((RWKV_W,), F32)])
        wup = jnp.concatenate([rwkv_w_up[l], lora_pad], axis=0).astype(BF16)
        aup = jnp.concatenate([lora_pad, rwkv_a_up[l]], axis=0).astype(BF16)
        o_rw = _rwkv(rw2.reshape(bsz, seq, RWKV_COLS), shift_mu[l][None], vec, wup, aup,
                     rwkv_g_up[l].astype(BF16), tril, hsum)

        wo = w_out[l].astype(BF16)
        x2 = _mlp(x2, o_att.reshape(t, ATT_W), conv2, o_rw.reshape(t, RWKV_W), conv_w[l],
                  wo[:ATT_W], wo[ATT_W:ATT_W + CONV_W], wo[ATT_W + CONV_W:], norm_mlp_g[l][None],
                  w_mlp_up[l].astype(BF16), w_mlp_down[l].astype(BF16), final_norm_g[None],
                  seq=seq, final=(l == depth - 1))
    return x2.reshape(bsz, seq, d)
```

```python
import functools
import math

import jax
import jax.numpy as jnp
from jax import lax
from jax.experimental import pallas as pl
from jax.experimental.pallas import tpu as pltpu

F32 = jnp.float32
BF16 = jnp.bfloat16

HEAD_DIM = 64
DIFF_HALF = HEAD_DIM // 2
ATT_HEADS = 6
CONV_GROUPS = 4
RWKV_HEADS = 6
ATT_W = ATT_HEADS * HEAD_DIM
CONV_W = CONV_GROUPS * HEAD_DIM
RWKV_W = RWKV_HEADS * HEAD_DIM
CONV_K = 3
DECAY_LORA = 64
ICLR_LORA = 64
GATE_LORA = 128
RWKV_COLS = 3 * RWKV_W + DECAY_LORA + ICLR_LORA + GATE_LORA
NORM_EPS = 1e-6
GN_EPS = 64e-5

LANES = 128
SUBLANES = 8
HEADS_PER_VREG = LANES // HEAD_DIM
VMEM_LIMIT_BYTES = 56 * 1024 * 1024

ROW_TILE = 512
ATT_TILE = 512
RWKV_CHUNK = 128
RWKV_ROWS = 2
FF_CHUNK = 1024
NEG_BIG = -1e30


def _compiler_params(n_axes):
    return pltpu.CompilerParams(
        dimension_semantics=("arbitrary",) * n_axes,
        vmem_limit_bytes=VMEM_LIMIT_BYTES,
    )


def _const_spec(shape):
    nd = len(shape)
    return pl.BlockSpec(shape, lambda *_: (0,) * nd)


def _mm(a, b):
    return jnp.dot(a.astype(BF16), b.astype(BF16), preferred_element_type=F32)


def _mm_nt(a, b):
    return lax.dot_general(a.astype(BF16), b.astype(BF16), (((1,), (1,)), ((), ())),
                           preferred_element_type=F32)


def _mm_tn(a, b):
    return lax.dot_general(a.astype(BF16), b.astype(BF16), (((0,), (0,)), ((), ())),
                           preferred_element_type=F32)


def _mm_pairs(lhs, rhs, transpose_rhs=False):
    assert len(lhs) % 2 == 0 and len(lhs) == len(rhs)
    out = []
    for n in range(0, len(lhs), 2):
        l0, l1, r0, r1 = (t.astype(BF16) for t in (lhs[n], lhs[n + 1], rhs[n], rhs[n + 1]))
        zero = jnp.zeros_like(r0)
        rhs_bd = jnp.concatenate([jnp.concatenate([r0, zero], axis=1),
                                  jnp.concatenate([zero, r1], axis=1)], axis=0)
        lhs_cat = jnp.concatenate([l0, l1], axis=1)
        both = _mm_nt(lhs_cat, rhs_bd) if transpose_rhs else _mm(lhs_cat, rhs_bd)
        split = both.shape[1] // 2
        out += [both[:, :split], both[:, split:]]
    return out


def _mm_split(x, ones_bf16, terms):
    acc = None
    rem = x
    for t in range(terms):
        piece = rem.astype(BF16)
        d = jnp.dot(piece, ones_bf16, preferred_element_type=F32)
        acc = d if acc is None else acc + d
        if t + 1 < terms:
            rem = rem - piece.astype(F32)
    return acc


def _cumsum_rows(tril_bf16, x):
    acc = None
    rem = x
    for t in range(3):
        piece = rem.astype(BF16)
        d = jnp.dot(tril_bf16, piece, preferred_element_type=F32)
        acc = d if acc is None else acc + d
        if t < 2:
            rem = rem - piece.astype(F32)
    return acc


def _rms(x, g):
    ms = jnp.mean(x * x, axis=-1, keepdims=True)
    return x * lax.rsqrt(ms + NORM_EPS) * g


def _inproj_kernel(x_ref, g_ref, qscale_ref, w_qk_ref, w_vt_ref, w_conv_ref, w_rw_ref,
                   qk_ref, vt_ref, conv_ref, rw_ref):
    h = _rms(x_ref[...], g_ref[...]).astype(BF16)
    qk = jnp.dot(h, w_qk_ref[...], preferred_element_type=F32)
    qk_ref[...] = (qk * qscale_ref[...]).astype(BF16)
    vt_ref[...] = _mm_nt(w_vt_ref[...], h).astype(BF16)
    conv_ref[...] = jnp.dot(h, w_conv_ref[...], preferred_element_type=F32)
    rw_ref[...] = jnp.dot(h, w_rw_ref[...], preferred_element_type=F32)


def _inproj(x2, g, qscale, w_qk, w_vt, w_conv, w_rw):
    t, d = x2.shape
    tm = ROW_TILE
    row = lambda n: pl.BlockSpec((tm, n), lambda i: (i, 0))
    return pl.pallas_call(
        _inproj_kernel,
        grid=(t // tm,),
        in_specs=[row(d), _const_spec(g.shape), _const_spec(qscale.shape), _const_spec(w_qk.shape),
                  _const_spec(w_vt.shape), _const_spec(w_conv.shape), _const_spec(w_rw.shape)],
        out_specs=[row(2 * ATT_W), pl.BlockSpec((ATT_W, tm), lambda i: (0, i)),
                   row(3 * CONV_W), row(RWKV_COLS)],
        out_shape=[jax.ShapeDtypeStruct((t, 2 * ATT_W), BF16),
                   jax.ShapeDtypeStruct((ATT_W, t), BF16),
                   jax.ShapeDtypeStruct((t, 3 * CONV_W), F32),
                   jax.ShapeDtypeStruct((t, RWKV_COLS), F32)],
        compiler_params=_compiler_params(1),
        name="inproj",
    )(x2, g, qscale, w_qk, w_vt, w_conv, w_rw)


def _attn_kernel(lam_ref, sg_ref, q_ref, k_ref, vt_ref, o_ref, *, lambda_init):
    tq = q_ref.shape[1]
    i = pl.program_id(2)
    q = q_ref[0]
    lane = lax.broadcasted_iota(jnp.int32, (tq, LANES), 1)
    n_maps = LANES // DIFF_HALF
    q_sel = [jnp.where(lane // DIFF_HALF == c, q, jnp.zeros_like(q)) for c in range(n_maps)]
    key_idx = lax.broadcasted_iota(jnp.int32, (tq, tq), 0)
    qry_idx = lax.broadcasted_iota(jnp.int32, (tq, tq), 1)
    causal = key_idx <= qry_idx

    def step(j, carry, diagonal):
        start = pl.multiple_of(j * tq, tq)
        ks = k_ref[0, pl.ds(start, tq), :]
        vt = vt_ref[:, pl.ds(start, tq)]
        scores = [_mm_nt(ks, q_sel[c]) for c in range(n_maps)]
        stats = []
        for c in range(n_maps):
            m, l, _ = carry[c]
            s = jnp.where(causal, scores[c], -jnp.inf) if diagonal else scores[c]
            m_new = jnp.maximum(m, jnp.max(s, axis=0, keepdims=True))
            alpha = jnp.exp2(m - m_new)
            p = jnp.exp2(s - m_new)
            l_new = alpha * l + jnp.sum(p, axis=0, keepdims=True)
            stats.append((m_new, l_new, alpha, p.astype(BF16)))
        out = []
        for c, (m_new, l_new, alpha, p) in enumerate(stats):
            hh = c // 2
            pv = jnp.dot(vt[hh * HEAD_DIM:(hh + 1) * HEAD_DIM], p, preferred_element_type=F32)
            out.append((m_new, l_new, alpha * carry[c][2] + pv))
        return tuple(out)

    init = tuple((jnp.full((1, tq), NEG_BIG, F32), jnp.zeros((1, tq), F32),
                  jnp.zeros((HEAD_DIM, tq), F32)) for _ in range(n_maps))
    carry = lax.fori_loop(0, i, functools.partial(step, diagonal=False), init)
    carry = step(i, carry, diagonal=True)

    lp = lam_ref[...]
    lam = (jnp.exp(jnp.sum(lp[0:1] * lp[1:2], axis=-1, keepdims=True))
           - jnp.exp(jnp.sum(lp[2:3] * lp[3:4], axis=-1, keepdims=True)) + lambda_init)
    heads = []
    for hh in range(HEADS_PER_VREG):
        (_, l0, a0), (_, l1, a1) = carry[2 * hh], carry[2 * hh + 1]
        o = a0 * (1.0 / l0) - lam * (a1 * (1.0 / l1))
        ms = jnp.mean(o * o, axis=0, keepdims=True)
        heads.append(o * lax.rsqrt(ms + NORM_EPS) * sg_ref[...] * (1.0 - lambda_init))
    o_ref[0] = jnp.concatenate(heads, axis=0).T.astype(o_ref.dtype)


def _attention(qk3, vt, lam_params, subln_col, lambda_init):
    b, s, _ = qk3.shape
    tq = ATT_TILE
    n_pairs = ATT_W // LANES
    return pl.pallas_call(
        functools.partial(_attn_kernel, lambda_init=lambda_init),
        grid=(b, n_pairs, s // tq),
        in_specs=[_const_spec(lam_params.shape), _const_spec(subln_col.shape),
                  pl.BlockSpec((1, tq, LANES), lambda bb, hp, i: (bb, i, hp)),
                  pl.BlockSpec((1, s, LANES), lambda bb, hp, i: (bb, 0, n_pairs + hp)),
                  pl.BlockSpec((LANES, s), lambda bb, hp, i: (hp, bb))],
        out_specs=pl.BlockSpec((1, tq, LANES), lambda bb, hp, i: (bb, i, hp)),
        out_shape=jax.ShapeDtypeStruct((b, s, ATT_W), BF16),
        compiler_params=_compiler_params(3),
        name="diff_attn",
    )(lam_params, subln_col, qk3, qk3, vt)


def _tri_inverse_all(mats, r_idx, c_idx):
    n = mats[0].shape[0]
    base = 16
    same = lambda blk: (r_idx // blk) == (c_idx // blk)
    eye = (r_idx == c_idx).astype(F32)
    in_base = same(base)
    powers = [jnp.where(in_base, a, 0.0) for a in mats]
    invs = [eye + d for d in powers]
    span = 2
    while span < base:
        powers = _mm_pairs(powers, powers)
        invs = [inv + t for inv, t in zip(invs, _mm_pairs(powers, invs))]
        span *= 2
    blk = base
    while blk < n:
        off_diag = jnp.logical_and(same(2 * blk), jnp.logical_not(same(blk)))
        tmp = _mm_pairs([jnp.where(off_diag, a, 0.0) for a in mats], invs)
        invs = [inv + t for inv, t in zip(invs, _mm_pairs(invs, tmp))]
        blk *= 2
    return invs


def _rwkv_kernel(p_ref, mu_ref, vec_ref, wup_ref, aup_ref, gup_ref, tril_ref, hsum_ref,
                 o_ref, state_ref, prev_ref):
    nb, c, _ = p_ref.shape
    w = RWKV_W

    @pl.when(pl.program_id(1) == 0)
    def _():
        state_ref[...] = jnp.zeros_like(state_ref)
        prev_ref[...] = jnp.zeros_like(prev_ref)

    p = jnp.concatenate([p_ref[n] for n in range(nb)], axis=0)
    row1 = lax.broadcasted_iota(jnp.int32, (nb * c, 1), 0)
    prev = pltpu.roll(p, 1, 0)
    for n in range(nb):
        prev = jnp.where(row1 == n * c, prev_ref[n, SUBLANES - 1:SUBLANES, :], prev)
        prev_ref[n] = p[(n + 1) * c - SUBLANES:(n + 1) * c, :]
    z = p + mu_ref[...] * (prev - p)

    vec = vec_ref[...]
    w0, a0, k_k, k_a, r_k, ln_g, ln_b = (vec[n:n + 1] for n in range(7))
    r = z[:, 0:w]
    k = z[:, w:2 * w]
    v = z[:, 2 * w:3 * w]
    lora_in = z[:, 3 * w:3 * w + DECAY_LORA + ICLR_LORA]
    g_in = z[:, 3 * w + DECAY_LORA + ICLR_LORA:]

    wdec = w0 + _mm(jnp.tanh(lora_in), wup_ref[...])
    logw = -math.exp(-0.5) * jax.nn.sigmoid(wdec)
    iclr = jax.nn.sigmoid(a0 + _mm(lora_in, aup_ref[...]))
    gate = _mm(jax.nn.sigmoid(g_in), gup_ref[...])

    hsum = hsum_ref[...]
    kk = k * k_k
    kk_norm = jnp.maximum(jnp.sqrt(_mm_split(kk * kk, hsum, 2)), 1e-12)
    kk = kk / kk_norm
    k = k * (1.0 + (iclr - 1.0) * k_a)
    a_vec = -kk
    b_vec = kk * iclr

    lcum_all = _cumsum_rows(tril_ref[...], logw)

    r_idx = lax.broadcasted_iota(jnp.int32, (c, c), 0)
    c_idx = lax.broadcasted_iota(jnp.int32, (c, c), 1)
    strict = c_idx < r_idx
    incl = c_idx <= r_idx
    strict2 = jnp.concatenate([strict, strict], axis=1)
    incl2 = jnp.concatenate([incl, incl], axis=1)
    lane_head = lax.broadcasted_iota(jnp.int32, (2 * c, LANES), 1) // HEAD_DIM
    lane_head_c = lax.broadcasted_iota(jnp.int32, (c, LANES), 1) // HEAD_DIM
    blk_r = lax.broadcasted_iota(jnp.int32, (LANES, LANES), 0) // HEAD_DIM
    blk_c = lax.broadcasted_iota(jnp.int32, (LANES, LANES), 1) // HEAD_DIM
    same_head = blk_r == blk_c

    n_pairs = w // LANES
    groups = [(n, pr) for n in range(nb) for pr in range(n_pairs)]
    heads = [(gi, hh) for gi in range(len(groups)) for hh in range(HEADS_PER_VREG)]
    ar, bk_chk, bk_hat, vp, g_mid, g_end = [], [], [], [], [], []
    for n in range(nb):
        rows = slice(n * c, (n + 1) * c)
        lcum = lcum_all[rows]
        lmid = lcum[c // 2 - 1:c // 2, :]
        lend = lcum[c - 1:c, :]
        lc = lcum - lmid
        g_inv = jnp.exp(-lc)
        g_tail = jnp.exp(lend - lcum)
        a_til = a_vec[rows] * jnp.exp(lc - logw[rows])
        r_til = r[rows] * jnp.exp(lc)
        b_chk = b_vec[rows] * g_inv
        k_chk = k[rows] * g_inv
        b_hat = b_vec[rows] * g_tail
        k_hat = k[rows] * g_tail
        gm, ge = jnp.exp(lmid), jnp.exp(lend)
        for pr in range(n_pairs):
            sl = slice(pr * LANES, (pr + 1) * LANES)
            ar.append(jnp.concatenate([a_til[:, sl], r_til[:, sl]], axis=0).astype(BF16))
            bk_chk.append(jnp.concatenate([b_chk[:, sl], k_chk[:, sl]], axis=0).astype(BF16))
            bk_hat.append(jnp.concatenate([b_hat[:, sl], k_hat[:, sl]], axis=0).astype(BF16))
            vp.append(v[rows, sl])
            g_mid.append(gm[:, sl])
            g_end.append(ge[:, sl])
    state = [state_ref[n, pr] for n, pr in groups]

    g = [_mm_nt(jnp.where(lane_head == hh, ar[gi], jnp.zeros_like(ar[gi])), bk_chk[gi])
         for gi, hh in heads]
    x = _mm_pairs(ar, [state[gi] * g_mid[gi] for gi in range(len(groups))], transpose_rhs=True)
    w_a = [jnp.where(strict2, gh[:c], 0.0) for gh in g]
    w_r = [jnp.where(incl2, gh[c:], 0.0) for gh in g]

    def both_heads(mats, rhs):
        split = jnp.concatenate([jnp.where(lane_head_c == hh, rhs, 0.0)
                                 for hh in range(HEADS_PER_VREG)], axis=0)
        return _mm(jnp.concatenate(mats, axis=1), split)

    ak_v = [both_heads([w_a[2 * gi][:, c:], w_a[2 * gi + 1][:, c:]], vp[gi])
            for gi in range(len(groups))]
    t_inv = _tri_inverse_all([wa[:, :c] for wa in w_a], r_idx, c_idx)
    u_pair = [both_heads(t_inv[2 * gi:2 * gi + 2], x[gi][:c] + ak_v[gi]) for gi in range(len(groups))]
    uv = [jnp.concatenate([u_pair[gi], vp[gi]], axis=0).astype(BF16) for gi in range(len(groups))]
    y_h = [x[gi][c:] + _mm(w_r[hi], uv[gi]) for hi, (gi, hh) in enumerate(heads)]
    for gi, (n, pr) in enumerate(groups):
        state_ref[n, pr] = (state[gi] * g_end[gi]
                            + jnp.where(same_head, _mm_tn(uv[gi], bk_hat[gi]), 0.0))
    y_pair = [jnp.where(lane_head_c == 0, y_h[2 * gi], y_h[2 * gi + 1]) for gi in range(len(groups))]
    y = jnp.concatenate([jnp.concatenate(y_pair[n * n_pairs:(n + 1) * n_pairs], axis=1)
                         for n in range(nb)], axis=0)
    mean = _mm_split(y, hsum, 2) * (1.0 / HEAD_DIM)
    yc = y - mean
    var = _mm_split(yc * yc, hsum, 2) * (1.0 / HEAD_DIM)
    yn = yc * lax.rsqrt(var + GN_EPS) * ln_g + ln_b
    bonus = _mm_split(r * k * r_k, hsum, 2) * v
    out = ((yn + bonus) * gate).astype(o_ref.dtype)
    for n in range(nb):
        o_ref[n] = out[n * c:(n + 1) * c]


def _rwkv(rw3, mu, vec, wup, aup, gup, tril, hsum):
    b, s, cols = rw3.shape
    c = RWKV_CHUNK
    nb = RWKV_ROWS
    return pl.pallas_call(
        _rwkv_kernel,
        grid=(b // nb, s // c),
        in_specs=[pl.BlockSpec((nb, c, cols), lambda bb, i: (bb, i, 0)),
                  _const_spec(mu.shape), _const_spec(vec.shape), _const_spec(wup.shape),
                  _const_spec(aup.shape), _const_spec(gup.shape), _const_spec(tril.shape),
                  _const_spec(hsum.shape)],
        out_specs=pl.BlockSpec((nb, c, RWKV_W), lambda bb, i: (bb, i, 0)),
        out_shape=jax.ShapeDtypeStruct((b, s, RWKV_W), BF16),
        scratch_shapes=[pltpu.VMEM((nb, RWKV_W // LANES, LANES, LANES), F32),
                        pltpu.VMEM((nb, SUBLANES, cols), F32)],
        compiler_params=_compiler_params(2),
        name="rwkv7",
    )(rw3, mu, vec, wup, aup, gup, tril, hsum)


def _mlp_kernel(x_ref, att_ref, conv_ref, halo_ref, rw_ref, convw_ref, wo_att_ref, wo_conv_ref,
                wo_rw_ref, g_ref, wup_ref, wdn_ref, gfin_ref, o_ref, *, tiles_per_seq, final):
    tm = x_ref.shape[0]
    cw = CONV_W
    conv = conv_ref[...]
    zc = conv[:, cw:2 * cw] * conv[:, 2 * cw:]
    halo = halo_ref[...]
    seq_start = (pl.program_id(0) % tiles_per_seq) == 0
    zh = jnp.where(seq_start, 0.0, halo[:, cw:2 * cw] * halo[:, 2 * cw:])
    row = lax.broadcasted_iota(jnp.int32, (tm, 1), 0)
    z1 = jnp.where(row == 0, zh[SUBLANES - 1:SUBLANES], pltpu.roll(zc, 1, 0))
    z2 = jnp.where(row == 0, zh[SUBLANES - 2:SUBLANES - 1],
                   jnp.where(row == 1, zh[SUBLANES - 1:SUBLANES], pltpu.roll(zc, 2, 0)))
    cwt = convw_ref[...]
    o_conv = conv[:, :cw] * (cwt[0:1] * z2 + cwt[1:2] * z1 + cwt[2:3] * zc)

    x = x_ref[...]
    x = x + (jnp.dot(att_ref[...], wo_att_ref[...], preferred_element_type=F32)
             + jnp.dot(o_conv.astype(BF16), wo_conv_ref[...], preferred_element_type=F32)
             + jnp.dot(rw_ref[...], wo_rw_ref[...], preferred_element_type=F32))

    h = _rms(x, g_ref[...]).astype(BF16)
    d_ff = wup_ref.shape[1]
    acc = x
    for j in range(d_ff // FF_CHUNK):
        cols = slice(j * FF_CHUNK, (j + 1) * FF_CHUNK)
        up = jnp.maximum(jnp.dot(h, wup_ref[:, cols], preferred_element_type=F32), 0.0)
        acc = acc + jnp.dot((up * up).astype(BF16), wdn_ref[cols, :], preferred_element_type=F32)
    if final:
        acc = _rms(acc, gfin_ref[...])
    o_ref[...] = acc


def _mlp(x2, att2, conv2, rw2, convw, wo_att, wo_conv, wo_rw, g, wup, wdn, gfin, *, seq, final):
    t, d = x2.shape
    tm = ROW_TILE
    halo_blocks = tm // SUBLANES
    row = lambda n: pl.BlockSpec((tm, n), lambda i: (i, 0))
    halo = pl.BlockSpec((SUBLANES, conv2.shape[1]),
                        lambda i: (jnp.maximum(i * halo_blocks - 1, 0), 0))
    consts = [convw, wo_att, wo_conv, wo_rw, g, wup, wdn, gfin]
    return pl.pallas_call(
        functools.partial(_mlp_kernel, tiles_per_seq=seq // tm, final=final),
        grid=(t // tm,),
        in_specs=[row(d), row(att2.shape[1]), row(conv2.shape[1]), halo, row(rw2.shape[1])]
                 + [_const_spec(a.shape) for a in consts],
        out_specs=row(d),
        out_shape=jax.ShapeDtypeStruct((t, d), F32),
        compiler_params=_compiler_params(1),
        name="mix_mlp",
    )(x2, att2, conv2, conv2, rw2, *consts)


def kernel(x, norm_mix_g, w_in, lam_q1, lam_k1, lam_q2, lam_k2, subln_g, conv_w, shift_mu, rwkv_w0, rwkv_w_up, rwkv_a0, rwkv_a_up, rwkv_g_up, rwkv_k_k, rwkv_k_a, rwkv_r_k, lnx_g, lnx_b, w_out, norm_mlp_g, w_mlp_up, w_mlp_down, final_norm_g):
    bsz, seq, d = x.shape
    depth = w_in.shape[0]
    t = bsz * seq
    assert t % ROW_TILE == 0 and seq % ROW_TILE == 0 and seq % ATT_TILE == 0 and seq % RWKV_CHUNK == 0
    assert bsz % RWKV_ROWS == 0

    att_cols, conv_cols = 3 * ATT_W, 3 * CONV_W
    qscale = jnp.concatenate([jnp.full((1, ATT_W), DIFF_HALF ** -0.5 * math.log2(math.e), F32),
                              jnp.ones((1, ATT_W), F32)], axis=1)
    idx = jnp.arange(RWKV_ROWS * RWKV_CHUNK)
    tril = jnp.logical_and(idx[None, :] <= idx[:, None],
                           idx[None, :] // RWKV_CHUNK == idx[:, None] // RWKV_CHUNK).astype(BF16)
    hidx = jnp.arange(RWKV_W) // HEAD_DIM
    hsum = (hidx[:, None] == hidx[None, :]).astype(BF16)
    lora_pad = jnp.zeros((DECAY_LORA, RWKV_W), F32)

    x2 = x.reshape(t, d)
    for l in range(depth):
        lambda_init = 0.8 - 0.6 * math.exp(-0.3 * l)
        w_l = w_in[l].astype(BF16)
        qk2, vt, conv2, rw2 = _inproj(
            x2, norm_mix_g[l][None], qscale, w_l[:, :2 * ATT_W], w_l[:, 2 * ATT_W:att_cols].T,
            w_l[:, att_cols:att_cols + conv_cols], w_l[:, att_cols + conv_cols:])

        lam_params = jnp.stack([lam_q1[l], lam_k1[l], lam_q2[l], lam_k2[l]])
        o_att = _attention(qk2.reshape(bsz, seq, 2 * ATT_W), vt, lam_params, subln_g[l][:, None],
                           lambda_init)

        vec = jnp.stack([rwkv_w0[l], rwkv_a0[l], rwkv_k_k[l], rwkv_k_a[l], rwkv_r_k[l].reshape(-1),
                         lnx_g[l], lnx_b[l], jnp.zeros((RWKV_W,), F32)])
        wup = jnp.concatenate([rwkv_w_up[l], lora_pad], axis=0).astype(BF16)
        aup = jnp.concatenate([lora_pad, rwkv_a_up[l]], axis=0).astype(BF16)
        o_rw = _rwkv(rw2.reshape(bsz, seq, RWKV_COLS), shift_mu[l][None], vec, wup, aup,
                     rwkv_g_up[l].astype(BF16), tril, hsum)

        wo = w_out[l].astype(BF16)
        x2 = _mlp(x2, o_att.reshape(t, ATT_W), conv2, o_rw.reshape(t, RWKV_W), conv_w[l],
                  wo[:ATT_W], wo[ATT_W:ATT_W + CONV_W], wo[ATT_W + CONV_W:], norm_mlp_g[l][None],
                  w_mlp_up[l].astype(BF16), w_mlp_down[l].astype(BF16), final_norm_g[None],
                  seq=seq, final=(l == depth - 1))
    return x2.reshape(bsz, seq, d)
```

```python
import functools
import math

import jax
import jax.numpy as jnp
from jax import lax
from jax.experimental import pallas as pl
from jax.experimental.pallas import tpu as pltpu

F32 = jnp.float32
BF16 = jnp.bfloat16

HEAD_DIM = 64
DIFF_HALF = HEAD_DIM // 2
ATT_HEADS = 6
CONV_GROUPS = 4
RWKV_HEADS = 6
ATT_W = ATT_HEADS * HEAD_DIM
CONV_W = CONV_GROUPS * HEAD_DIM
RWKV_W = RWKV_HEADS * HEAD_DIM
CONV_K = 3
DECAY_LORA = 64
ICLR_LORA = 64
GATE_LORA = 128
RWKV_COLS = 3 * RWKV_W + DECAY_LORA + ICLR_LORA + GATE_LORA
NORM_EPS = 1e-6
GN_EPS = 64e-5

LANES = 128
SUBLANES = 8
HEADS_PER_VREG = LANES // HEAD_DIM
VMEM_LIMIT_BYTES = 56 * 1024 * 1024

ROW_TILE = 512
ATT_TILE = 512
RWKV_CHUNK = 64
RWKV_ROWS = 8
CUMSUM_BLOCK = 256
FF_CHUNK = 1024
NEG_BIG = -1e30


def _compiler_params(n_axes):
    return pltpu.CompilerParams(
        dimension_semantics=("arbitrary",) * n_axes,
        vmem_limit_bytes=VMEM_LIMIT_BYTES,
    )


def _const_spec(shape):
    nd = len(shape)
    return pl.BlockSpec(shape, lambda *_: (0,) * nd)


def _mm(a, b):
    return jnp.dot(a.astype(BF16), b.astype(BF16), preferred_element_type=F32)


def _mm_nt(a, b):
    return lax.dot_general(a.astype(BF16), b.astype(BF16), (((1,), (1,)), ((), ())),
                           preferred_element_type=F32)


def _mm_tn(a, b):
    return lax.dot_general(a.astype(BF16), b.astype(BF16), (((0,), (0,)), ((), ())),
                           preferred_element_type=F32)


def _bf16_pieces(x, terms):
    pieces = []
    rem = x
    for t in range(terms):
        piece = rem.astype(BF16)
        pieces.append(piece)
        if t + 1 < terms:
            rem = rem - piece.astype(F32)
    return pieces


def _head_sums(x, ones_bf16, terms):
    out = []
    for s in range(x.shape[1] // LANES):
        xs = x[:, s * LANES:(s + 1) * LANES]
        out.append(functools.reduce(lambda a, b: a + b, [
            jnp.dot(piece, ones_bf16, preferred_element_type=F32) for piece in _bf16_pieces(xs, terms)]))
    return jnp.concatenate(out, axis=1)


def _cumsum_rows(tril_bf16, x):
    blk = tril_bf16.shape[0]
    out = []
    for s in range(x.shape[0] // blk):
        xs = x[s * blk:(s + 1) * blk]
        out.append(functools.reduce(lambda a, b: a + b, [
            jnp.dot(tril_bf16, piece, preferred_element_type=F32) for piece in _bf16_pieces(xs, 3)]))
    return jnp.concatenate(out, axis=0)


def _rms(x, g):
    ms = jnp.mean(x * x, axis=-1, keepdims=True)
    return x * lax.rsqrt(ms + NORM_EPS) * g


def _inproj_kernel(x_ref, g_ref, qscale_ref, w_qk_ref, w_vt_ref, w_conv_ref, w_rw_ref,
                   qk_ref, vt_ref, conv_ref, rw_ref):
    h = _rms(x_ref[...], g_ref[...]).astype(BF16)
    qk = jnp.dot(h, w_qk_ref[...], preferred_element_type=F32)
    qk_ref[...] = (qk * qscale_ref[...]).astype(BF16)
    vt_ref[...] = _mm_nt(w_vt_ref[...], h).astype(BF16)
    conv_ref[...] = jnp.dot(h, w_conv_ref[...], preferred_element_type=F32)
    rw_ref[...] = jnp.dot(h, w_rw_ref[...], preferred_element_type=F32)


def _inproj(x2, g, qscale, w_qk, w_vt, w_conv, w_rw):
    t, d = x2.shape
    tm = ROW_TILE
    row = lambda n: pl.BlockSpec((tm, n), lambda i: (i, 0))
    return pl.pallas_call(
        _inproj_kernel,
        grid=(t // tm,),
        in_specs=[row(d), _const_spec(g.shape), _const_spec(qscale.shape), _const_spec(w_qk.shape),
                  _const_spec(w_vt.shape), _const_spec(w_conv.shape), _const_spec(w_rw.shape)],
        out_specs=[row(2 * ATT_W), pl.BlockSpec((ATT_W, tm), lambda i: (0, i)),
                   row(3 * CONV_W), row(RWKV_COLS)],
        out_shape=[jax.ShapeDtypeStruct((t, 2 * ATT_W), BF16),
                   jax.ShapeDtypeStruct((ATT_W, t), BF16),
                   jax.ShapeDtypeStruct((t, 3 * CONV_W), F32),
                   jax.ShapeDtypeStruct((t, RWKV_COLS), F32)],
        compiler_params=_compiler_params(1),
        name="inproj",
    )(x2, g, qscale, w_qk, w_vt, w_conv, w_rw)


def _attn_kernel(lam_ref, sg_ref, q_ref, k_ref, vt_ref, o_ref, *, lambda_init):
    tq = q_ref.shape[1]
    i = pl.program_id(2)
    q = q_ref[0]
    lane = lax.broadcasted_iota(jnp.int32, (tq, LANES), 1)
    n_maps = LANES // DIFF_HALF
    q_sel = [jnp.where(lane // DIFF_HALF == c, q, jnp.zeros_like(q)) for c in range(n_maps)]
    key_idx = lax.broadcasted_iota(jnp.int32, (tq, tq), 0)
    qry_idx = lax.broadcasted_iota(jnp.int32, (tq, tq), 1)
    causal = key_idx <= qry_idx

    def step(j, carry, diagonal):
        start = pl.multiple_of(j * tq, tq)
        ks = k_ref[0, pl.ds(start, tq), :]
        vt = vt_ref[:, pl.ds(start, tq)]
        scores = [_mm_nt(ks, q_sel[c]) for c in range(n_maps)]
        stats = []
        for c in range(n_maps):
            m, l, _ = carry[c]
            s = jnp.where(causal, scores[c], -jnp.inf) if diagonal else scores[c]
            m_new = jnp.maximum(m, jnp.max(s, axis=0, keepdims=True))
            alpha = jnp.exp2(m - m_new)
            p = jnp.exp2(s - m_new)
            l_new = alpha * l + jnp.sum(p, axis=0, keepdims=True)
            stats.append((m_new, l_new, alpha, p.astype(BF16)))
        out = []
        for c, (m_new, l_new, alpha, p) in enumerate(stats):
            hh = c // 2
            pv = jnp.dot(vt[hh * HEAD_DIM:(hh + 1) * HEAD_DIM], p, preferred_element_type=F32)
            out.append((m_new, l_new, alpha * carry[c][2] + pv))
        return tuple(out)

    init = tuple((jnp.full((1, tq), NEG_BIG, F32), jnp.zeros((1, tq), F32),
                  jnp.zeros((HEAD_DIM, tq), F32)) for _ in range(n_maps))
    carry = lax.fori_loop(0, i, functools.partial(step, diagonal=False), init)
    carry = step(i, carry, diagonal=True)

    lp = lam_ref[...]
    lam = (jnp.exp(jnp.sum(lp[0:1] * lp[1:2], axis=-1, keepdims=True))
           - jnp.exp(jnp.sum(lp[2:3] * lp[3:4], axis=-1, keepdims=True)) + lambda_init)
    heads = []
    for hh in range(HEADS_PER_VREG):
        (_, l0, a0), (_, l1, a1) = carry[2 * hh], carry[2 * hh + 1]
        o = a0 * (1.0 / l0) - lam * (a1 * (1.0 / l1))
        ms = jnp.mean(o * o, axis=0, keepdims=True)
        heads.append(o * lax.rsqrt(ms + NORM_EPS) * sg_ref[...] * (1.0 - lambda_init))
    o_ref[0] = jnp.concatenate(heads, axis=0).T.astype(o_ref.dtype)


def _attention(qk3, vt, lam_params, subln_col, lambda_init):
    b, s, _ = qk3.shape
    tq = ATT_TILE
    n_pairs = ATT_W // LANES
    return pl.pallas_call(
        functools.partial(_attn_kernel, lambda_init=lambda_init),
        grid=(b, n_pairs, s // tq),
        in_specs=[_const_spec(lam_params.shape), _const_spec(subln_col.shape),
                  pl.BlockSpec((1, tq, LANES), lambda bb, hp, i: (bb, i, hp)),
                  pl.BlockSpec((1, s, LANES), lambda bb, hp, i: (bb, 0, n_pairs + hp)),
                  pl.BlockSpec((LANES, s), lambda bb, hp, i: (hp, bb))],
        out_specs=pl.BlockSpec((1, tq, LANES), lambda bb, hp, i: (bb, i, hp)),
        out_shape=jax.ShapeDtypeStruct((b, s, ATT_W), BF16),
        compiler_params=_compiler_params(3),
        name="diff_attn",
    )(lam_params, subln_col, qk3, qk3, vt)


def _head_split(x, lane_head):
    return jnp.concatenate([jnp.where(lane_head == hh, x, jnp.zeros_like(x))
                            for hh in range(HEADS_PER_VREG)], axis=0)


def _tri_inverse_all(mats, r_idx, c_idx, lane_head):
    c = mats[0].shape[0]
    base = 16
    same = lambda blk: (r_idx // blk) == (c_idx // blk)
    eye = (r_idx == c_idx).astype(F32)
    mul = lambda ls, rs: [_mm(l, _head_split(r, lane_head)) for l, r in zip(ls, rs)]
    in_base = same(base)
    powers = [jnp.where(in_base, a, 0.0) for a in mats]
    invs = [eye + d for d in powers]
    span = 2
    while span < base:
        powers = mul(powers, powers)
        invs = [inv + t for inv, t in zip(invs, mul(powers, invs))]
        span *= 2
    blk = base
    while blk < c:
        off_diag = jnp.logical_and(same(2 * blk), jnp.logical_not(same(blk)))
        tmp = mul([jnp.where(off_diag, a, 0.0) for a in mats], invs)
        invs = [inv + t for inv, t in zip(invs, mul(invs, tmp))]
        blk *= 2
    return invs


def _rwkv_kernel(p_ref, mu_ref, vec_ref, wup_ref, aup_ref, gup_ref, tril_ref, hsum_ref,
                 o_ref, state_ref, prev_ref):
    nb, c, _ = p_ref.shape
    w = RWKV_W
    assert HEADS_PER_VREG * c == LANES

    @pl.when(pl.program_id(1) == 0)
    def _():
        state_ref[...] = jnp.zeros_like(state_ref)
        prev_ref[...] = jnp.zeros_like(prev_ref)

    p = jnp.concatenate([p_ref[n] for n in range(nb)], axis=0)
    row1 = lax.broadcasted_iota(jnp.int32, (nb * c, 1), 0)
    prev = pltpu.roll(p, 1, 0)
    for n in range(nb):
        prev = jnp.where(row1 == n * c, prev_ref[n, SUBLANES - 1:SUBLANES, :], prev)
        prev_ref[n] = p[(n + 1) * c - SUBLANES:(n + 1) * c, :]
    z = p + mu_ref[...] * (prev - p)

    vec = vec_ref[...]
    w0, a0, k_k, k_a, r_k, ln_g, ln_b = (vec[n:n + 1] for n in range(7))
    r = z[:, 0:w]
    k = z[:, w:2 * w]
    v = z[:, 2 * w:3 * w]
    lora_in = z[:, 3 * w:3 * w + DECAY_LORA + ICLR_LORA]
    g_in = z[:, 3 * w + DECAY_LORA + ICLR_LORA:]

    wdec = w0 + _mm(jnp.tanh(lora_in), wup_ref[...])
    logw = -math.exp(-0.5) * jax.nn.sigmoid(wdec)
    iclr = jax.nn.sigmoid(a0 + _mm(lora_in, aup_ref[...]))
    gate = _mm(jax.nn.sigmoid(g_in), gup_ref[...])

    hsum = hsum_ref[...]
    kk = k * k_k
    kk_norm = jnp.maximum(jnp.sqrt(_head_sums(kk * kk, hsum, 1)), 1e-12)
    kk = kk / kk_norm
    k = k * (1.0 + (iclr - 1.0) * k_a)
    a_vec = -kk
    b_vec = kk * iclr
    lcum_all = _cumsum_rows(tril_ref[...], logw)

    r_idx = lax.broadcasted_iota(jnp.int32, (c, LANES), 0)
    lane = lax.broadcasted_iota(jnp.int32, (c, LANES), 1)
    c_idx = lane % c
    lane_head = lane // HEAD_DIM
    strict = c_idx < r_idx
    incl = c_idx <= r_idx
    strict2 = jnp.concatenate([strict, strict], axis=1)
    incl2 = jnp.concatenate([incl, incl], axis=1)
    blk_r = lax.broadcasted_iota(jnp.int32, (LANES, LANES), 0) // HEAD_DIM
    blk_c = lax.broadcasted_iota(jnp.int32, (LANES, LANES), 1) // HEAD_DIM
    same_head = blk_r == blk_c

    n_pairs = w // LANES
    groups = [(n, pr) for n in range(nb) for pr in range(n_pairs)]
    ar, bk_chk, bk_hat, vp, g_mid, g_end = [], [], [], [], [], []
    for n in range(nb):
        rows = slice(n * c, (n + 1) * c)
        lcum = lcum_all[rows]
        lmid = lcum[c // 2 - 1:c // 2, :]
        lend = lcum[c - 1:c, :]
        lc = lcum - lmid
        g_inv = jnp.exp(-lc)
        g_tail = jnp.exp(lend - lcum)
        a_til = a_vec[rows] * jnp.exp(lc - logw[rows])
        r_til = r[rows] * jnp.exp(lc)
        b_chk = b_vec[rows] * g_inv
        k_chk = k[rows] * g_inv
        b_hat = b_vec[rows] * g_tail
        k_hat = k[rows] * g_tail
        gm, ge = jnp.exp(lmid), jnp.exp(lend)
        for pr in range(n_pairs):
            sl = slice(pr * LANES, (pr + 1) * LANES)
            ar.append(jnp.concatenate([a_til[:, sl], r_til[:, sl]], axis=0).astype(BF16))
            bk_chk.append(jnp.concatenate([_head_split(b_chk[:, sl].astype(BF16), lane_head),
                                           _head_split(k_chk[:, sl].astype(BF16), lane_head)], axis=0))
            bk_hat.append(jnp.concatenate([b_hat[:, sl], k_hat[:, sl]], axis=0).astype(BF16))
            vp.append(v[rows, sl])
            g_mid.append(gm[:, sl])
            g_end.append(ge[:, sl])
    v_split = [_head_split(t.astype(BF16), lane_head) for t in vp]
    state = [state_ref[n, pr] for n, pr in groups]

    g = [_mm_nt(ar[gi], bk_chk[gi]) for gi in range(len(groups))]
    x = [_mm_nt(ar[gi], state[gi] * g_mid[gi]) for gi in range(len(groups))]
    w_a = [jnp.where(strict2, gh[:c], 0.0) for gh in g]
    w_r = [jnp.where(incl2, gh[c:], 0.0).astype(BF16) for gh in g]
    ak_v = [_mm(w_a[gi][:, LANES:], v_split[gi]) for gi in range(len(groups))]
    t_inv = _tri_inverse_all([wa[:, :LANES] for wa in w_a], r_idx, c_idx, lane_head)
    u = [_mm(t_inv[gi], _head_split(x[gi][:c] + ak_v[gi], lane_head)) for gi in range(len(groups))]
    y_pair = [x[gi][c:] + _mm(w_r[gi], jnp.concatenate(
                  [_head_split(u[gi].astype(BF16), lane_head), v_split[gi]], axis=0))
              for gi in range(len(groups))]
    for gi, (n, pr) in enumerate(groups):
        uv = jnp.concatenate([u[gi], vp[gi]], axis=0)
        state_ref[n, pr] = (state[gi] * g_end[gi]
                            + jnp.where(same_head, _mm_tn(uv, bk_hat[gi]), 0.0))
    y = jnp.concatenate([jnp.concatenate(y_pair[n * n_pairs:(n + 1) * n_pairs], axis=1)
                         for n in range(nb)], axis=0)
    mean = _head_sums(y, hsum, 2) * (1.0 / HEAD_DIM)
    yc = y - mean
    var = _head_sums(yc * yc, hsum, 2) * (1.0 / HEAD_DIM)
    yn = yc * lax.rsqrt(var + GN_EPS) * ln_g + ln_b
    bonus = _head_sums(r * k * r_k, hsum, 1) * v
    out = ((yn + bonus) * gate).astype(o_ref.dtype)
    for n in range(nb):
        o_ref[n] = out[n * c:(n + 1) * c]


def _rwkv(rw3, mu, vec, wup, aup, gup, tril, hsum):
    b, s, cols = rw3.shape
    c = RWKV_CHUNK
    nb = RWKV_ROWS
    return pl.pallas_call(
        _rwkv_kernel,
        grid=(b // nb, s // c),
        in_specs=[pl.BlockSpec((nb, c, cols), lambda bb, i: (bb, i, 0)),
                  _const_spec(mu.shape), _const_spec(vec.shape), _const_spec(wup.shape),
                  _const_spec(aup.shape), _const_spec(gup.shape), _const_spec(tril.shape),
                  _const_spec(hsum.shape)],
        out_specs=pl.BlockSpec((nb, c, RWKV_W), lambda bb, i: (bb, i, 0)),
        out_shape=jax.ShapeDtypeStruct((b, s, RWKV_W), BF16),
        scratch_shapes=[pltpu.VMEM((nb, RWKV_W // LANES, LANES, LANES), F32),
                        pltpu.VMEM((nb, SUBLANES, cols), F32)],
        compiler_params=_compiler_params(2),
        name="rwkv7",
    )(rw3, mu, vec, wup, aup, gup, tril, hsum)


def _mlp_kernel(x_ref, att_ref, conv_ref, halo_ref, rw_ref, convw_ref, wo_att_ref, wo_conv_ref,
                wo_rw_ref, g_ref, wup_ref, wdn_ref, gfin_ref, o_ref, *, tiles_per_seq, final):
    tm = x_ref.shape[0]
    cw = CONV_W
    conv = conv_ref[...]
    zc = conv[:, cw:2 * cw] * conv[:, 2 * cw:]
    halo = halo_ref[...]
    seq_start = (pl.program_id(0) % tiles_per_seq) == 0
    zh = jnp.where(seq_start, 0.0, halo[:, cw:2 * cw] * halo[:, 2 * cw:])
    row = lax.broadcasted_iota(jnp.int32, (tm, 1), 0)
    z1 = jnp.where(row == 0, zh[SUBLANES - 1:SUBLANES], pltpu.roll(zc, 1, 0))
    z2 = jnp.where(row == 0, zh[SUBLANES - 2:SUBLANES - 1],
                   jnp.where(row == 1, zh[SUBLANES - 1:SUBLANES], pltpu.roll(zc, 2, 0)))
    cwt = convw_ref[...]
    o_conv = conv[:, :cw] * (cwt[0:1] * z2 + cwt[1:2] * z1 + cwt[2:3] * zc)

    x = x_ref[...]
    x = x + (jnp.dot(att_ref[...], wo_att_ref[...], preferred_element_type=F32)
             + jnp.dot(o_conv.astype(BF16), wo_conv_ref[...], preferred_element_type=F32)
             + jnp.dot(rw_ref[...], wo_rw_ref[...], preferred_element_type=F32))

    h = _rms(x, g_ref[...]).astype(BF16)
    d_ff = wup_ref.shape[1]
    acc = x
    for j in range(d_ff // FF_CHUNK):
        cols = slice(j * FF_CHUNK, (j + 1) * FF_CHUNK)
        up = jnp.maximum(jnp.dot(h, wup_ref[:, cols], preferred_element_type=F32), 0.0)
        acc = acc + jnp.dot((up * up).astype(BF16), wdn_ref[cols, :], preferred_element_type=F32)
    if final:
        acc = _rms(acc, gfin_ref[...])
    o_ref[...] = acc


def _mlp(x2, att2, conv2, rw2, convw, wo_att, wo_conv, wo_rw, g, wup, wdn, gfin, *, seq, final):
    t, d = x2.shape
    tm = ROW_TILE
    halo_blocks = tm // SUBLANES
    row = lambda n: pl.BlockSpec((tm, n), lambda i: (i, 0))
    halo = pl.BlockSpec((SUBLANES, conv2.shape[1]),
                        lambda i: (jnp.maximum(i * halo_blocks - 1, 0), 0))
    consts = [convw, wo_att, wo_conv, wo_rw, g, wup, wdn, gfin]
    return pl.pallas_call(
        functools.partial(_mlp_kernel, tiles_per_seq=seq // tm, final=final),
        grid=(t // tm,),
        in_specs=[row(d), row(att2.shape[1]), row(conv2.shape[1]), halo, row(rw2.shape[1])]
                 + [_const_spec(a.shape) for a in consts],
        out_specs=row(d),
        out_shape=jax.ShapeDtypeStruct((t, d), F32),
        compiler_params=_compiler_params(1),
        name="mix_mlp",
    )(x2, att2, conv2, conv2, rw2, *consts)


def kernel(x, norm_mix_g, w_in, lam_q1, lam_k1, lam_q2, lam_k2, subln_g, conv_w, shift_mu, rwkv_w0, rwkv_w_up, rwkv_a0, rwkv_a_up, rwkv_g_up, rwkv_k_k, rwkv_k_a, rwkv_r_k, lnx_g, lnx_b, w_out, norm_mlp_g, w_mlp_up, w_mlp_down, final_norm_g):
    bsz, seq, d = x.shape
    depth = w_in.shape[0]
    t = bsz * seq
    assert t % ROW_TILE == 0 and seq % ROW_TILE == 0 and seq % ATT_TILE == 0 and seq % RWKV_CHUNK == 0
    assert bsz % RWKV_ROWS == 0

    att_cols, conv_cols = 3 * ATT_W, 3 * CONV_W
    qscale = jnp.concatenate([jnp.full((1, ATT_W), DIFF_HALF ** -0.5 * math.log2(math.e), F32),
                              jnp.ones((1, ATT_W), F32)], axis=1)
    idx = jnp.arange(min(CUMSUM_BLOCK, RWKV_ROWS * RWKV_CHUNK))
    tril = jnp.logical_and(idx[None, :] <= idx[:, None],
                           idx[None, :] // RWKV_CHUNK == idx[:, None] // RWKV_CHUNK).astype(BF16)
    hidx = jnp.arange(LANES) // HEAD_DIM
    hsum = (hidx[:, None] == hidx[None, :]).astype(BF16)
    lora_pad = jnp.zeros((DECAY_LORA, RWKV_W), F32)

    x2 = x.reshape(t, d)
    for l in range(depth):
        lambda_init = 0.8 - 0.6 * math.exp(-0.3 * l)
        w_l = w_in[l].astype(BF16)
        qk2, vt, conv2, rw2 = _inproj(
            x2, norm_mix_g[l][None], qscale, w_l[:, :2 * ATT_W], w_l[:, 2 * ATT_W:att_cols].T,
            w_l[:, att_cols:att_cols + conv_cols], w_l[:, att_cols + conv_cols:])

        lam_params = jnp.stack([lam_q1[l], lam_k1[l], lam_q2[l], lam_k2[l]])
        o_att = _attention(qk2.reshape(bsz, seq, 2 * ATT_W), vt, lam_params, subln_g[l][:, None],
                           lambda_init)

        vec = jnp.stack([rwkv_w0[l], rwkv_a0[l], rwkv_k_k[l], rwkv_k_a[l], rwkv_r_k[l].reshape(-1),
                         lnx_g[l], lnx_b[l], jnp.zeros((RWKV_W,), F32)])
        wup = jnp.concatenate([rwkv_w_up[l], lora_pad], axis=0).astype(BF16)
        aup = jnp.concatenate([lora_pad, rwkv_a_up[l]], axis=0).astype(BF16)
        o_rw = _rwkv(rw2.reshape(bsz, seq, RWKV_COLS), shift_mu[l][None], vec, wup, aup,
                     rwkv_g_up[l].astype(BF16), tril, hsum)

        wo = w_out[l].astype(BF16)
        x2 = _mlp(x2, o_att.reshape(t, ATT_W), conv2, o_rw.reshape(t, RWKV_W), conv_w[l],
                  wo[:ATT_W], wo[ATT_W:ATT_W + CONV_W], wo[ATT_W + CONV_W:], norm_mlp_g[l][None],
                  w_mlp_up[l].astype(BF16), w_mlp_down[l].astype(BF16), final_norm_g[None],
                  seq=seq, final=(l == depth - 1))
    return x2.reshape(bsz, seq, d)
```

```python
import functools
import math

import jax
import jax.numpy as jnp
from jax import lax
from jax.experimental import pallas as pl
from jax.experimental.pallas import tpu as pltpu

F32 = jnp.float32
BF16 = jnp.bfloat16

HEAD_DIM = 64
DIFF_HALF = HEAD_DIM // 2
ATT_HEADS = 6
CONV_GROUPS = 4
RWKV_HEADS = 6
ATT_W = ATT_HEADS * HEAD_DIM
CONV_W = CONV_GROUPS * HEAD_DIM
RWKV_W = RWKV_HEADS * HEAD_DIM
CONV_K = 3
DECAY_LORA = 64
ICLR_LORA = 64
GATE_LORA = 128
RWKV_COLS = 3 * RWKV_W + DECAY_LORA + ICLR_LORA + GATE_LORA
NORM_EPS = 1e-6
GN_EPS = 64e-5

LANES = 128
SUBLANES = 8
HEADS_PER_VREG = LANES // HEAD_DIM
VMEM_LIMIT_BYTES = 56 * 1024 * 1024

ROW_TILE = 512
ATT_TILE = 512
RWKV_CHUNK = 64
RWKV_ROWS = 8
CUMSUM_BLOCK = 256
FF_CHUNK = 1024
NEG_BIG = -1e30
ONES_ROWS = 16


def _compiler_params(n_axes):
    return pltpu.CompilerParams(
        dimension_semantics=("arbitrary",) * n_axes,
        vmem_limit_bytes=VMEM_LIMIT_BYTES,
    )


def _const_spec(shape):
    nd = len(shape)
    return pl.BlockSpec(shape, lambda *_: (0,) * nd)


def _mm(a, b):
    return jnp.dot(a.astype(BF16), b.astype(BF16), preferred_element_type=F32)


def _mm_nt(a, b):
    return lax.dot_general(a.astype(BF16), b.astype(BF16), (((1,), (1,)), ((), ())),
                           preferred_element_type=F32)


def _mm_tn(a, b):
    return lax.dot_general(a.astype(BF16), b.astype(BF16), (((0,), (0,)), ((), ())),
                           preferred_element_type=F32)


def _bf16_pieces(x, terms):
    pieces = []
    rem = x
    for t in range(terms):
        piece = rem.astype(BF16)
        pieces.append(piece)
        if t + 1 < terms:
            rem = rem - piece.astype(F32)
    return pieces


def _head_sums(x, ones_bf16, terms):
    out = []
    for s in range(x.shape[1] // LANES):
        xs = x[:, s * LANES:(s + 1) * LANES]
        out.append(functools.reduce(lambda a, b: a + b, [
            jnp.dot(piece, ones_bf16, preferred_element_type=F32) for piece in _bf16_pieces(xs, terms)]))
    return jnp.concatenate(out, axis=1)


def _cumsum_rows(tril_bf16, x):
    blk = tril_bf16.shape[0]
    out = []
    for s in range(x.shape[0] // blk):
        xs = x[s * blk:(s + 1) * blk]
        out.append(functools.reduce(lambda a, b: a + b, [
            jnp.dot(tril_bf16, piece, preferred_element_type=F32) for piece in _bf16_pieces(xs, 3)]))
    return jnp.concatenate(out, axis=0)


def _sigmoid(x):
    return 0.5 * jnp.tanh(0.5 * x) + 0.5


def _rms(x, g):
    ms = jnp.mean(x * x, axis=-1, keepdims=True)
    return x * lax.rsqrt(ms + NORM_EPS) * g


def _inproj_kernel(x_ref, g_ref, qscale_ref, w_qk_ref, w_vt_ref, w_conv_ref, w_rw_ref,
                   qk_ref, vt_ref, conv_ref, rw_ref):
    h = _rms(x_ref[...], g_ref[...]).astype(BF16)
    qk = jnp.dot(h, w_qk_ref[...], preferred_element_type=F32)
    qk_ref[...] = (qk * qscale_ref[...]).astype(BF16)
    vt_ref[...] = _mm_nt(w_vt_ref[...], h).astype(BF16)
    conv_ref[...] = jnp.dot(h, w_conv_ref[...], preferred_element_type=F32)
    rw_ref[...] = jnp.dot(h, w_rw_ref[...], preferred_element_type=F32)


def _inproj(x2, g, qscale, w_qk, w_vt, w_conv, w_rw):
    t, d = x2.shape
    tm = ROW_TILE
    row = lambda n: pl.BlockSpec((tm, n), lambda i: (i, 0))
    return pl.pallas_call(
        _inproj_kernel,
        grid=(t // tm,),
        in_specs=[row(d), _const_spec(g.shape), _const_spec(qscale.shape), _const_spec(w_qk.shape),
                  _const_spec(w_vt.shape), _const_spec(w_conv.shape), _const_spec(w_rw.shape)],
        out_specs=[row(2 * ATT_W), pl.BlockSpec((ATT_W, tm), lambda i: (0, i)),
                   row(3 * CONV_W), row(RWKV_COLS)],
        out_shape=[jax.ShapeDtypeStruct((t, 2 * ATT_W), BF16),
                   jax.ShapeDtypeStruct((ATT_W, t), BF16),
                   jax.ShapeDtypeStruct((t, 3 * CONV_W), F32),
                   jax.ShapeDtypeStruct((t, RWKV_COLS), F32)],
        compiler_params=_compiler_params(1),
        name="inproj",
    )(x2, g, qscale, w_qk, w_vt, w_conv, w_rw)


def _attn_kernel(lam_ref, sg_ref, q_ref, k_ref, vt_ref, o_ref, *, lambda_init):
    tq = q_ref.shape[1]
    i = pl.program_id(2)
    q = q_ref[0]
    lane = lax.broadcasted_iota(jnp.int32, (tq, LANES), 1)
    n_maps = LANES // DIFF_HALF
    q_sel = [jnp.where(lane // DIFF_HALF == c, q, jnp.zeros_like(q)) for c in range(n_maps)]

    def step(k_start, tk, carry, q_lo=0, k_off=None):
        nq = tq - q_lo
        ks = k_ref[0, pl.ds(k_start, tk), :]
        vt = vt_ref[:, pl.ds(k_start, tk)]
        ones = jnp.ones((ONES_ROWS, tk), BF16)
        vt_aug = [jnp.concatenate([vt[hh * HEAD_DIM:(hh + 1) * HEAD_DIM], ones], axis=0)
                  for hh in range(HEADS_PER_VREG)]
        scores = [_mm_nt(ks, q_sel[c][q_lo:]) for c in range(n_maps)]
        if k_off is not None:
            key_pos = lax.broadcasted_iota(jnp.int32, (tk, nq), 0) + k_off
            qry_pos = lax.broadcasted_iota(jnp.int32, (tk, nq), 1) + q_lo
            scores = [jnp.where(key_pos <= qry_pos, s, -jnp.inf) for s in scores]
        stats = []
        for c in range(n_maps):
            m = carry[c][0][:, q_lo:]
            m_new = jnp.maximum(m, jnp.max(scores[c], axis=0, keepdims=True))
            stats.append((m_new, jnp.exp2(m - m_new), jnp.exp2(scores[c] - m_new).astype(BF16)))
        out = []
        for c, (m_new, alpha, p) in enumerate(stats):
            m_old, l_old, acc_old = carry[c]
            pv = jnp.dot(vt_aug[c // 2], p, preferred_element_type=F32)
            l_new = alpha * l_old[:, q_lo:] + pv[HEAD_DIM:HEAD_DIM + 1]
            acc_new = alpha * acc_old[:, q_lo:] + pv[:HEAD_DIM]
            if q_lo:
                m_new = jnp.concatenate([m_old[:, :q_lo], m_new], axis=1)
                l_new = jnp.concatenate([l_old[:, :q_lo], l_new], axis=1)
                acc_new = jnp.concatenate([acc_old[:, :q_lo], acc_new], axis=1)
            out.append((m_new, l_new, acc_new))
        return tuple(out)

    init = tuple((jnp.full((1, tq), NEG_BIG, F32), jnp.zeros((1, tq), F32),
                  jnp.zeros((HEAD_DIM, tq), F32)) for _ in range(n_maps))
    carry = lax.fori_loop(0, i, lambda j, cr: step(pl.multiple_of(j * tq, tq), tq, cr), init)
    half = tq // 2
    diag = pl.multiple_of(i * tq, tq)
    carry = step(diag, half, carry, q_lo=0, k_off=0)
    carry = step(diag + half, half, carry, q_lo=half, k_off=half)

    lp = lam_ref[...]
    lam = (jnp.exp(jnp.sum(lp[0:1] * lp[1:2], axis=-1, keepdims=True))
           - jnp.exp(jnp.sum(lp[2:3] * lp[3:4], axis=-1, keepdims=True)) + lambda_init)
    heads = []
    for hh in range(HEADS_PER_VREG):
        (_, l0, a0), (_, l1, a1) = carry[2 * hh], carry[2 * hh + 1]
        o = a0 * (1.0 / l0) - lam * (a1 * (1.0 / l1))
        ms = jnp.mean(o * o, axis=0, keepdims=True)
        heads.append(o * lax.rsqrt(ms + NORM_EPS) * sg_ref[...] * (1.0 - lambda_init))
    o_ref[0] = jnp.concatenate(heads, axis=0).T.astype(o_ref.dtype)


def _attention(qk3, vt, lam_params, subln_col, lambda_init):
    b, s, _ = qk3.shape
    tq = ATT_TILE
    n_pairs = ATT_W // LANES
    return pl.pallas_call(
        functools.partial(_attn_kernel, lambda_init=lambda_init),
        grid=(b, n_pairs, s // tq),
        in_specs=[_const_spec(lam_params.shape), _const_spec(subln_col.shape),
                  pl.BlockSpec((1, tq, LANES), lambda bb, hp, i: (bb, i, hp)),
                  pl.BlockSpec((1, s, LANES), lambda bb, hp, i: (bb, 0, n_pairs + hp)),
                  pl.BlockSpec((LANES, s), lambda bb, hp, i: (hp, bb))],
        out_specs=pl.BlockSpec((1, tq, LANES), lambda bb, hp, i: (bb, i, hp)),
        out_shape=jax.ShapeDtypeStruct((b, s, ATT_W), BF16),
        compiler_params=_compiler_params(3),
        name="diff_attn",
    )(lam_params, subln_col, qk3, qk3, vt)


def _head_split(x, lane_head):
    return jnp.concatenate([jnp.where(lane_head == hh, x, jnp.zeros_like(x))
                            for hh in range(HEADS_PER_VREG)], axis=0)


def _tri_inverse_all(mats, r_idx, c_idx, lane_head):
    c = mats[0].shape[0]
    base = 16
    same = lambda blk: (r_idx // blk) == (c_idx // blk)
    eye = (r_idx == c_idx).astype(F32)
    mul = lambda ls, rs: [_mm(l, _head_split(r, lane_head)) for l, r in zip(ls, rs)]
    in_base = same(base)
    powers = [jnp.where(in_base, a, 0.0) for a in mats]
    invs = [eye + d for d in powers]
    span = 2
    while span < base:
        powers = mul(powers, powers)
        invs = [inv + t for inv, t in zip(invs, mul(powers, invs))]
        span *= 2
    blk = base
    while blk < c:
        off_diag = jnp.logical_and(same(2 * blk), jnp.logical_not(same(blk)))
        tmp = mul([jnp.where(off_diag, a, 0.0) for a in mats], invs)
        invs = [inv + t for inv, t in zip(invs, mul(invs, tmp))]
        blk *= 2
    return invs


def _rwkv_kernel(p_ref, mu_ref, vec_ref, wup_ref, aup_ref, gup_ref, tril_ref, hsum_ref,
                 o_ref, state_ref, prev_ref):
    nb, c, _ = p_ref.shape
    w = RWKV_W
    assert HEADS_PER_VREG * c == LANES

    @pl.when(pl.program_id(1) == 0)
    def _():
        state_ref[...] = jnp.zeros_like(state_ref)
        prev_ref[...] = jnp.zeros_like(prev_ref)

    p = jnp.concatenate([p_ref[n] for n in range(nb)], axis=0)
    row1 = lax.broadcasted_iota(jnp.int32, (nb * c, 1), 0)
    prev = pltpu.roll(p, 1, 0)
    for n in range(nb):
        prev = jnp.where(row1 == n * c, prev_ref[n, SUBLANES - 1:SUBLANES, :], prev)
        prev_ref[n] = p[(n + 1) * c - SUBLANES:(n + 1) * c, :]
    z = p + mu_ref[...] * (prev - p)

    vec = vec_ref[...]
    w0, a0, k_k, k_a, r_k, ln_g, ln_b = (vec[n:n + 1] for n in range(7))
    r = z[:, 0:w]
    k = z[:, w:2 * w]
    v = z[:, 2 * w:3 * w]
    lora_in = z[:, 3 * w:3 * w + DECAY_LORA + ICLR_LORA]
    g_in = z[:, 3 * w + DECAY_LORA + ICLR_LORA:]

    wdec = w0 + _mm(jnp.tanh(lora_in), wup_ref[...])
    logw = -(math.exp(-0.5) * math.log2(math.e)) * _sigmoid(wdec)
    iclr = _sigmoid(a0 + _mm(lora_in, aup_ref[...]))
    gate = _mm(_sigmoid(g_in), gup_ref[...])

    hsum = hsum_ref[...]
    kk = k * k_k
    kk = kk * lax.rsqrt(jnp.maximum(_head_sums(kk * kk, hsum, 1), 1e-24))
    k = k * (1.0 + (iclr - 1.0) * k_a)
    a_vec = -kk
    b_vec = kk * iclr
    lcum_all = _cumsum_rows(tril_ref[...], logw)

    r_idx = lax.broadcasted_iota(jnp.int32, (c, LANES), 0)
    lane = lax.broadcasted_iota(jnp.int32, (c, LANES), 1)
    c_idx = lane % c
    lane_head = lane // HEAD_DIM
    strict = c_idx < r_idx
    incl = c_idx <= r_idx
    strict2 = jnp.concatenate([strict, strict], axis=1)
    incl2 = jnp.concatenate([incl, incl], axis=1)
    blk_r = lax.broadcasted_iota(jnp.int32, (LANES, LANES), 0) // HEAD_DIM
    blk_c = lax.broadcasted_iota(jnp.int32, (LANES, LANES), 1) // HEAD_DIM
    same_head = blk_r == blk_c

    n_pairs = w // LANES
    groups = [(n, pr) for n in range(nb) for pr in range(n_pairs)]
    ar, bk_chk, bk_hat, vp, g_mid, g_end = [], [], [], [], [], []
    for n in range(nb):
        rows = slice(n * c, (n + 1) * c)
        lcum = lcum_all[rows]
        lmid = lcum[c // 2 - 1:c // 2, :]
        lend = lcum[c - 1:c, :]
        lc = lcum - lmid
        g_inv = jnp.exp2(-lc)
        g_tail = jnp.exp2(lend - lcum)
        a_til = a_vec[rows] * jnp.exp2(lc - logw[rows])
        r_til = r[rows] * jnp.exp2(lc)
        b_chk = b_vec[rows] * g_inv
        k_chk = k[rows] * g_inv
        b_hat = b_vec[rows] * g_tail
        k_hat = k[rows] * g_tail
        gm, ge = jnp.exp2(lmid), jnp.exp2(lend)
        for pr in range(n_pairs):
            sl = slice(pr * LANES, (pr + 1) * LANES)
            ar.append(jnp.concatenate([a_til[:, sl], r_til[:, sl]], axis=0).astype(BF16))
            bk_chk.append(jnp.concatenate([_head_split(b_chk[:, sl].astype(BF16), lane_head),
                                           _head_split(k_chk[:, sl].astype(BF16), lane_head)], axis=0))
            bk_hat.append(jnp.concatenate([b_hat[:, sl], k_hat[:, sl]], axis=0).astype(BF16))
            vp.append(v[rows, sl])
            g_mid.append(gm[:, sl])
            g_end.append(ge[:, sl])
    v_split = [_head_split(t.astype(BF16), lane_head) for t in vp]
    state = [state_ref[n, pr] for n, pr in groups]

    g = [_mm_nt(ar[gi], bk_chk[gi]) for gi in range(len(groups))]
    x = [_mm_nt(ar[gi], state[gi] * g_mid[gi]) for gi in range(len(groups))]
    w_a = [jnp.where(strict2, gh[:c], 0.0) for gh in g]
    w_r = [jnp.where(incl2, gh[c:], 0.0).astype(BF16) for gh in g]
    ak_v = [_mm(w_a[gi][:, LANES:], v_split[gi]) for gi in range(len(groups))]
    t_inv = _tri_inverse_all([wa[:, :LANES] for wa in w_a], r_idx, c_idx, lane_head)
    u = [_mm(t_inv[gi], _head_split(x[gi][:c] + ak_v[gi], lane_head)) for gi in range(len(groups))]
    y_pair = [x[gi][c:] + _mm(w_r[gi], jnp.concatenate(
                  [_head_split(u[gi].astype(BF16), lane_head), v_split[gi]], axis=0))
              for gi in range(len(groups))]
    for gi, (n, pr) in enumerate(groups):
        uv = jnp.concatenate([u[gi], vp[gi]], axis=0)
        state_ref[n, pr] = (state[gi] * g_end[gi]
                            + jnp.where(same_head, _mm_tn(uv, bk_hat[gi]), 0.0))
    y = jnp.concatenate([jnp.concatenate(y_pair[n * n_pairs:(n + 1) * n_pairs], axis=1)
                         for n in range(nb)], axis=0)
    mean = _head_sums(y, hsum, 2) * (1.0 / HEAD_DIM)
    yc = y - mean
    var = _head_sums(yc * yc, hsum, 2) * (1.0 / HEAD_DIM)
    yn = yc * lax.rsqrt(var + GN_EPS) * ln_g + ln_b
    bonus = _head_sums(r * k * r_k, hsum, 1) * v
    out = ((yn + bonus) * gate).astype(o_ref.dtype)
    for n in range(nb):
        o_ref[n] = out[n * c:(n + 1) * c]


def _rwkv(rw3, mu, vec, wup, aup, gup, tril, hsum):
    b, s, cols = rw3.shape
    c = RWKV_CHUNK
    nb = RWKV_ROWS
    return pl.pallas_call(
        _rwkv_kernel,
        grid=(b // nb, s // c),
        in_specs=[pl.BlockSpec((nb, c, cols), lambda bb, i: (bb, i, 0)),
                  _const_spec(mu.shape), _const_spec(vec.shape), _const_spec(wup.shape),
                  _const_spec(aup.shape), _const_spec(gup.shape), _const_spec(tril.shape),
                  _const_spec(hsum.shape)],
        out_specs=pl.BlockSpec((nb, c, RWKV_W), lambda bb, i: (bb, i, 0)),
        out_shape=jax.ShapeDtypeStruct((b, s, RWKV_W), BF16),
        scratch_shapes=[pltpu.VMEM((nb, RWKV_W // LANES, LANES, LANES), F32),
                        pltpu.VMEM((nb, SUBLANES, cols), F32)],
        compiler_params=_compiler_params(2),
        name="rwkv7",
    )(rw3, mu, vec, wup, aup, gup, tril, hsum)


def _mlp_kernel(x_ref, att_ref, conv_ref, halo_ref, rw_ref, convw_ref, wo_att_ref, wo_conv_ref,
                wo_rw_ref, g_ref, wup_ref, wdn_ref, gfin_ref, o_ref, *, tiles_per_seq, final):
    tm = x_ref.shape[0]
    cw = CONV_W
    conv = conv_ref[...]
    zc = conv[:, cw:2 * cw] * conv[:, 2 * cw:]
    halo = halo_ref[...]
    seq_start = (pl.program_id(0) % tiles_per_seq) == 0
    zh = jnp.where(seq_start, 0.0, halo[:, cw:2 * cw] * halo[:, 2 * cw:])
    row = lax.broadcasted_iota(jnp.int32, (tm, 1), 0)
    z1 = jnp.where(row == 0, zh[SUBLANES - 1:SUBLANES], pltpu.roll(zc, 1, 0))
    z2 = jnp.where(row == 0, zh[SUBLANES - 2:SUBLANES - 1],
                   jnp.where(row == 1, zh[SUBLANES - 1:SUBLANES], pltpu.roll(zc, 2, 0)))
    cwt = convw_ref[...]
    o_conv = conv[:, :cw] * (cwt[0:1] * z2 + cwt[1:2] * z1 + cwt[2:3] * zc)

    x = x_ref[...]
    x = x + (jnp.dot(att_ref[...], wo_att_ref[...], preferred_element_type=F32)
             + jnp.dot(o_conv.astype(BF16), wo_conv_ref[...], preferred_element_type=F32)
             + jnp.dot(rw_ref[...], wo_rw_ref[...], preferred_element_type=F32))

    h = _rms(x, g_ref[...]).astype(BF16)
    d_ff = wup_ref.shape[1]
    acc = x
    for j in range(d_ff // FF_CHUNK):
        cols = slice(j * FF_CHUNK, (j + 1) * FF_CHUNK)
        up = jnp.maximum(jnp.dot(h, wup_ref[:, cols], preferred_element_type=F32), 0.0)
        acc = acc + jnp.dot((up * up).astype(BF16), wdn_ref[cols, :], preferred_element_type=F32)
    if final:
        acc = _rms(acc, gfin_ref[...])
    o_ref[...] = acc


def _mlp(x2, att2, conv2, rw2, convw, wo_att, wo_conv, wo_rw, g, wup, wdn, gfin, *, seq, final):
    t, d = x2.shape
    tm = ROW_TILE
    halo_blocks = tm // SUBLANES
    row = lambda n: pl.BlockSpec((tm, n), lambda i: (i, 0))
    halo = pl.BlockSpec((SUBLANES, conv2.shape[1]),
                        lambda i: (jnp.maximum(i * halo_blocks - 1, 0), 0))
    consts = [convw, wo_att, wo_conv, wo_rw, g, wup, wdn, gfin]
    return pl.pallas_call(
        functools.partial(_mlp_kernel, tiles_per_seq=seq // tm, final=final),
        grid=(t // tm,),
        in_specs=[row(d), row(att2.shape[1]), row(conv2.shape[1]), halo, row(rw2.shape[1])]
                 + [_const_spec(a.shape) for a in consts],
        out_specs=row(d),
        out_shape=jax.ShapeDtypeStruct((t, d), F32),
        compiler_params=_compiler_params(1),
        name="mix_mlp",
    )(x2, att2, conv2, conv2, rw2, *consts)


def kernel(x, norm_mix_g, w_in, lam_q1, lam_k1, lam_q2, lam_k2, subln_g, conv_w, shift_mu, rwkv_w0, rwkv_w_up, rwkv_a0, rwkv_a_up, rwkv_g_up, rwkv_k_k, rwkv_k_a, rwkv_r_k, lnx_g, lnx_b, w_out, norm_mlp_g, w_mlp_up, w_mlp_down, final_norm_g):
    bsz, seq, d = x.shape
    depth = w_in.shape[0]
    t = bsz * seq
    assert t % ROW_TILE == 0 and seq % ROW_TILE == 0 and seq % ATT_TILE == 0 and seq % RWKV_CHUNK == 0
    assert bsz % RWKV_ROWS == 0

    att_cols, conv_cols = 3 * ATT_W, 3 * CONV_W
    qscale = jnp.concatenate([jnp.full((1, ATT_W), DIFF_HALF ** -0.5 * math.log2(math.e), F32),
                              jnp.ones((1, ATT_W), F32)], axis=1)
    idx = jnp.arange(min(CUMSUM_BLOCK, RWKV_ROWS * RWKV_CHUNK))
    tril = jnp.logical_and(idx[None, :] <= idx[:, None],
                           idx[None, :] // RWKV_CHUNK == idx[:, None] // RWKV_CHUNK).astype(BF16)
    hidx = jnp.arange(LANES) // HEAD_DIM
    hsum = (hidx[:, None] == hidx[None, :]).astype(BF16)
    lora_pad = jnp.zeros((DECAY_LORA, RWKV_W), F32)

    x2 = x.reshape(t, d)
    for l in range(depth):
        lambda_init = 0.8 - 0.6 * math.exp(-0.3 * l)
        w_l = w_in[l].astype(BF16)
        qk2, vt, conv2, rw2 = _inproj(
            x2, norm_mix_g[l][None], qscale, w_l[:, :2 * ATT_W], w_l[:, 2 * ATT_W:att_cols].T,
            w_l[:, att_cols:att_cols + conv_cols], w_l[:, att_cols + conv_cols:])

        lam_params = jnp.stack([lam_q1[l], lam_k1[l], lam_q2[l], lam_k2[l]])
        o_att = _attention(qk2.reshape(bsz, seq, 2 * ATT_W), vt, lam_params, subln_g[l][:, None],
                           lambda_init)

        vec = jnp.stack([rwkv_w0[l], rwkv_a0[l], rwkv_k_k[l], rwkv_k_a[l], rwkv_r_k[l].reshape(-1),
                         lnx_g[l], lnx_b[l], jnp.zeros((RWKV_W,), F32)])
        wup = jnp.concatenate([rwkv_w_up[l], lora_pad], axis=0).astype(BF16)
        aup = jnp.concatenate([lora_pad, rwkv_a_up[l]], axis=0).astype(BF16)
        o_rw = _rwkv(rw2.reshape(bsz, seq, RWKV_COLS), shift_mu[l][None], vec, wup, aup,
                     rwkv_g_up[l].astype(BF16), tril, hsum)

        wo = w_out[l].astype(BF16)
        x2 = _mlp(x2, o_att.reshape(t, ATT_W), conv2, o_rw.reshape(t, RWKV_W), conv_w[l],
                  wo[:ATT_W], wo[ATT_W:ATT_W + CONV_W], wo[ATT_W + CONV_W:], norm_mlp_g[l][None],
                  w_mlp_up[l].astype(BF16), w_mlp_down[l].astype(BF16), final_norm_g[None],
                  seq=seq, final=(l == depth - 1))
    return x2.reshape(bsz, seq, d)
```

```python
import functools
import math

import jax
import jax.numpy as jnp
from jax import lax
from jax.experimental import pallas as pl
from jax.experimental.pallas import tpu as pltpu

F32 = jnp.float32
BF16 = jnp.bfloat16

HEAD_DIM = 64
DIFF_HALF = HEAD_DIM // 2
ATT_HEADS = 6
CONV_GROUPS = 4
RWKV_HEADS = 6
ATT_W = ATT_HEADS * HEAD_DIM
CONV_W = CONV_GROUPS * HEAD_DIM
RWKV_W = RWKV_HEADS * HEAD_DIM
CONV_K = 3
DECAY_LORA = 64
ICLR_LORA = 64
GATE_LORA = 128
RWKV_COLS = 3 * RWKV_W + DECAY_LORA + ICLR_LORA + GATE_LORA
NORM_EPS = 1e-6
GN_EPS = 64e-5

LANES = 128
SUBLANES = 8
HEADS_PER_VREG = LANES // HEAD_DIM
VMEM_LIMIT_BYTES = 56 * 1024 * 1024

ROW_TILE = 512
ATT_TILE = 1024
RWKV_CHUNK = 64
RWKV_ROWS = 8
CUMSUM_BLOCK = 256
FF_CHUNK = 1024
NEG_BIG = -1e30
ONES_ROWS = 16


def _compiler_params(n_axes):
    return pltpu.CompilerParams(
        dimension_semantics=("arbitrary",) * n_axes,
        vmem_limit_bytes=VMEM_LIMIT_BYTES,
    )


def _const_spec(shape):
    nd = len(shape)
    return pl.BlockSpec(shape, lambda *_: (0,) * nd)


def _mm(a, b):
    return jnp.dot(a.astype(BF16), b.astype(BF16), preferred_element_type=F32)


def _mm_nt(a, b):
    return lax.dot_general(a.astype(BF16), b.astype(BF16), (((1,), (1,)), ((), ())),
                           preferred_element_type=F32)


def _mm_tn(a, b):
    return lax.dot_general(a.astype(BF16), b.astype(BF16), (((0,), (0,)), ((), ())),
                           preferred_element_type=F32)


def _bf16_pieces(x, terms):
    pieces = []
    rem = x
    for t in range(terms):
        piece = rem.astype(BF16)
        pieces.append(piece)
        if t + 1 < terms:
            rem = rem - piece.astype(F32)
    return pieces


def _head_sums(x, ones_bf16, terms):
    out = []
    for s in range(x.shape[1] // LANES):
        xs = x[:, s * LANES:(s + 1) * LANES]
        out.append(functools.reduce(lambda a, b: a + b, [
            jnp.dot(piece, ones_bf16, preferred_element_type=F32) for piece in _bf16_pieces(xs, terms)]))
    return jnp.concatenate(out, axis=1)


def _cumsum_rows(tril_bf16, x):
    blk = tril_bf16.shape[0]
    out = []
    for s in range(x.shape[0] // blk):
        xs = x[s * blk:(s + 1) * blk]
        out.append(functools.reduce(lambda a, b: a + b, [
            jnp.dot(tril_bf16, piece, preferred_element_type=F32) for piece in _bf16_pieces(xs, 3)]))
    return jnp.concatenate(out, axis=0)


def _sigmoid(x):
    return 0.5 * jnp.tanh(0.5 * x) + 0.5


def _rms(x, g):
    ms = jnp.mean(x * x, axis=-1, keepdims=True)
    return x * lax.rsqrt(ms + NORM_EPS) * g


def _inproj_kernel(x_ref, g_ref, qscale_ref, w_qk_ref, w_vt_ref, w_conv_ref, w_rw_ref,
                   qk_ref, vt_ref, conv_ref, rw_ref):
    h = _rms(x_ref[...], g_ref[...]).astype(BF16)
    qk = jnp.dot(h, w_qk_ref[...], preferred_element_type=F32)
    qk_ref[...] = (qk * qscale_ref[...]).astype(BF16)
    vt_ref[...] = _mm_nt(w_vt_ref[...], h).astype(BF16)
    conv_ref[...] = jnp.dot(h, w_conv_ref[...], preferred_element_type=F32)
    rw_ref[...] = jnp.dot(h, w_rw_ref[...], preferred_element_type=F32)


def _inproj(x2, g, qscale, w_qk, w_vt, w_conv, w_rw):
    t, d = x2.shape
    tm = ROW_TILE
    row = lambda n: pl.BlockSpec((tm, n), lambda i: (i, 0))
    return pl.pallas_call(
        _inproj_kernel,
        grid=(t // tm,),
        in_specs=[row(d), _const_spec(g.shape), _const_spec(qscale.shape), _const_spec(w_qk.shape),
                  _const_spec(w_vt.shape), _const_spec(w_conv.shape), _const_spec(w_rw.shape)],
        out_specs=[row(2 * ATT_W), pl.BlockSpec((ATT_W, tm), lambda i: (0, i)),
                   row(3 * CONV_W), row(RWKV_COLS)],
        out_shape=[jax.ShapeDtypeStruct((t, 2 * ATT_W), BF16),
                   jax.ShapeDtypeStruct((ATT_W, t), BF16),
                   jax.ShapeDtypeStruct((t, 3 * CONV_W), F32),
                   jax.ShapeDtypeStruct((t, RWKV_COLS), F32)],
        compiler_params=_compiler_params(1),
        name="inproj",
    )(x2, g, qscale, w_qk, w_vt, w_conv, w_rw)


def _attn_kernel(lam_ref, sg_ref, q_ref, k_ref, vt_ref, o_ref, *, lambda_init):
    tq = q_ref.shape[1]
    i = pl.program_id(2)
    q = q_ref[0]
    lane = lax.broadcasted_iota(jnp.int32, (tq, LANES), 1)
    n_maps = LANES // DIFF_HALF
    maps = range(n_maps)
    q_sel = [jnp.where(lane // DIFF_HALF == c, q, jnp.zeros_like(q)) for c in maps]

    def scores(k_start, tk, q_lo=0, k_off=None):
        ks = k_ref[0, pl.ds(k_start, tk), :]
        out = [_mm_nt(ks, q_sel[c][q_lo:]) for c in maps]
        if k_off is not None:
            key_pos = lax.broadcasted_iota(jnp.int32, (tk, tq - q_lo), 0) + k_off
            qry_pos = lax.broadcasted_iota(jnp.int32, (tk, tq - q_lo), 1) + q_lo
            out = [jnp.where(key_pos <= qry_pos, s, -jnp.inf) for s in out]
        return out

    def softmax(s, m):
        m_new = [jnp.maximum(m[c], jnp.max(s[c], axis=0, keepdims=True)) for c in maps]
        alpha = [jnp.exp2(m[c] - m_new[c]) for c in maps]
        p = [jnp.exp2(s[c] - m_new[c]).astype(BF16) for c in maps]
        return m_new, alpha, p

    def values(k_start, tk, p):
        vt = vt_ref[:, pl.ds(k_start, tk)]
        ones = jnp.ones((ONES_ROWS, tk), BF16)
        vt_aug = [jnp.concatenate([vt[hh * HEAD_DIM:(hh + 1) * HEAD_DIM], ones], axis=0)
                  for hh in range(HEADS_PER_VREG)]
        return [jnp.dot(vt_aug[c // 2], p[c], preferred_element_type=F32) for c in maps]

    def fold(k_start, tk, carry, q_lo=0, k_off=None):
        m, l, acc = carry
        m_new, alpha, p = softmax(scores(k_start, tk, q_lo, k_off), [m[c][:, q_lo:] for c in maps])
        pv = values(k_start, tk, p)
        l_new = [alpha[c] * l[c][:, q_lo:] + pv[c][HEAD_DIM:HEAD_DIM + 1] for c in maps]
        acc_new = [alpha[c] * acc[c][:, q_lo:] + pv[c][:HEAD_DIM] for c in maps]
        if q_lo:
            keep = lambda old, new: [jnp.concatenate([old[c][:, :q_lo], new[c]], axis=1) for c in maps]
            m_new, l_new, acc_new = keep(m, m_new), keep(l, l_new), keep(acc, acc_new)
        return m_new, l_new, acc_new

    carry = ([jnp.full((1, tq), NEG_BIG, F32) for _ in maps], [jnp.zeros((1, tq), F32) for _ in maps],
             [jnp.zeros((HEAD_DIM, tq), F32) for _ in maps])
    carry = lax.fori_loop(0, i, lambda j, cr: fold(pl.multiple_of(j * tq, tq), tq, cr), carry)
    half = tq // 2
    diag = pl.multiple_of(i * tq, tq)
    carry = fold(diag, half, carry, q_lo=0, k_off=0)
    _, l, acc = fold(diag + half, half, carry, q_lo=half, k_off=half)

    lp = lam_ref[...]
    lam = (jnp.exp(jnp.sum(lp[0:1] * lp[1:2], axis=-1, keepdims=True))
           - jnp.exp(jnp.sum(lp[2:3] * lp[3:4], axis=-1, keepdims=True)) + lambda_init)
    heads = []
    for hh in range(HEADS_PER_VREG):
        o = acc[2 * hh] * (1.0 / l[2 * hh]) - lam * (acc[2 * hh + 1] * (1.0 / l[2 * hh + 1]))
        ms = jnp.mean(o * o, axis=0, keepdims=True)
        heads.append(o * lax.rsqrt(ms + NORM_EPS) * sg_ref[...] * (1.0 - lambda_init))
    o_ref[0] = jnp.concatenate(heads, axis=0).T.astype(o_ref.dtype)


def _attention(qk3, vt, lam_params, subln_col, lambda_init):
    b, s, _ = qk3.shape
    tq = ATT_TILE
    n_pairs = ATT_W // LANES
    return pl.pallas_call(
        functools.partial(_attn_kernel, lambda_init=lambda_init),
        grid=(b, n_pairs, s // tq),
        in_specs=[_const_spec(lam_params.shape), _const_spec(subln_col.shape),
                  pl.BlockSpec((1, tq, LANES), lambda bb, hp, i: (bb, i, hp)),
                  pl.BlockSpec((1, s, LANES), lambda bb, hp, i: (bb, 0, n_pairs + hp)),
                  pl.BlockSpec((LANES, s), lambda bb, hp, i: (hp, bb))],
        out_specs=pl.BlockSpec((1, tq, LANES), lambda bb, hp, i: (bb, i, hp)),
        out_shape=jax.ShapeDtypeStruct((b, s, ATT_W), BF16),
        compiler_params=_compiler_params(3),
        name="diff_attn",
    )(lam_params, subln_col, qk3, qk3, vt)


def _head_split(x, lane_head):
    return jnp.concatenate([jnp.where(lane_head == hh, x, jnp.zeros_like(x))
                            for hh in range(HEADS_PER_VREG)], axis=0)


def _tri_inverse_all(mats, r_idx, c_idx, lane_head):
    c = mats[0].shape[0]
    base = 16
    same = lambda blk: (r_idx // blk) == (c_idx // blk)
    eye = (r_idx == c_idx).astype(F32)
    mul = lambda ls, rs: [_mm(l, _head_split(r, lane_head)) for l, r in zip(ls, rs)]
    in_base = same(base)
    powers = [jnp.where(in_base, a, 0.0) for a in mats]
    invs = [eye + d for d in powers]
    span = 2
    while span < base:
        powers = mul(powers, powers)
        invs = [inv + t for inv, t in zip(invs, mul(powers, invs))]
        span *= 2
    blk = base
    while blk < c:
        off_diag = jnp.logical_and(same(2 * blk), jnp.logical_not(same(blk)))
        tmp = mul([jnp.where(off_diag, a, 0.0) for a in mats], invs)
        invs = [inv + t for inv, t in zip(invs, mul(invs, tmp))]
        blk *= 2
    return invs


def _rwkv_kernel(p_ref, mu_ref, vec_ref, wup_ref, aup_ref, gup_ref, tril_ref, hsum_ref,
                 o_ref, state_ref, prev_ref):
    nb, c, _ = p_ref.shape
    w = RWKV_W
    assert HEADS_PER_VREG * c == LANES

    @pl.when(pl.program_id(1) == 0)
    def _():
        state_ref[...] = jnp.zeros_like(state_ref)
        prev_ref[...] = jnp.zeros_like(prev_ref)

    p = jnp.concatenate([p_ref[n] for n in range(nb)], axis=0)
    row1 = lax.broadcasted_iota(jnp.int32, (nb * c, 1), 0)
    prev = pltpu.roll(p, 1, 0)
    for n in range(nb):
        prev = jnp.where(row1 == n * c, prev_ref[n, SUBLANES - 1:SUBLANES, :], prev)
        prev_ref[n] = p[(n + 1) * c - SUBLANES:(n + 1) * c, :]
    z = p + mu_ref[...] * (prev - p)

    vec = vec_ref[...]
    w0, a0, k_k, k_a, r_k, ln_g, ln_b = (vec[n:n + 1] for n in range(7))
    r = z[:, 0:w]
    k = z[:, w:2 * w]
    v = z[:, 2 * w:3 * w]
    lora_in = z[:, 3 * w:3 * w + DECAY_LORA + ICLR_LORA]
    g_in = z[:, 3 * w + DECAY_LORA + ICLR_LORA:]

    wdec = w0 + _mm(jnp.tanh(lora_in), wup_ref[...])
    logw = -(math.exp(-0.5) * math.log2(math.e)) * _sigmoid(wdec)
    iclr = _sigmoid(a0 + _mm(lora_in, aup_ref[...]))
    gate = _mm(_sigmoid(g_in), gup_ref[...])

    hsum = hsum_ref[...]
    kk = k * k_k
    kk = kk * lax.rsqrt(jnp.maximum(_head_sums(kk * kk, hsum, 1), 1e-24))
    k = k * (1.0 + (iclr - 1.0) * k_a)
    a_vec = -kk
    b_vec = kk * iclr
    lcum_all = _cumsum_rows(tril_ref[...], logw)

    r_idx = lax.broadcasted_iota(jnp.int32, (c, LANES), 0)
    lane = lax.broadcasted_iota(jnp.int32, (c, LANES), 1)
    c_idx = lane % c
    lane_head = lane // HEAD_DIM
    strict = c_idx < r_idx
    incl = c_idx <= r_idx
    strict2 = jnp.concatenate([strict, strict], axis=1)
    incl2 = jnp.concatenate([incl, incl], axis=1)
    blk_r = lax.broadcasted_iota(jnp.int32, (LANES, LANES), 0) // HEAD_DIM
    blk_c = lax.broadcasted_iota(jnp.int32, (LANES, LANES), 1) // HEAD_DIM
    same_head = blk_r == blk_c

    n_pairs = w // LANES
    groups = [(n, pr) for n in range(nb) for pr in range(n_pairs)]
    ar, bk_chk, bk_hat, vp, g_mid, g_end = [], [], [], [], [], []
    for n in range(nb):
        rows = slice(n * c, (n + 1) * c)
        lcum = lcum_all[rows]
        lmid = lcum[c // 2 - 1:c // 2, :]
        lend = lcum[c - 1:c, :]
        lc = lcum - lmid
        g_inv = jnp.exp2(-lc)
        g_tail = jnp.exp2(lend - lcum)
        a_til = a_vec[rows] * jnp.exp2(lc - logw[rows])
        r_til = r[rows] * jnp.exp2(lc)
        b_chk = b_vec[rows] * g_inv
        k_chk = k[rows] * g_inv
        b_hat = b_vec[rows] * g_tail
        k_hat = k[rows] * g_tail
        gm, ge = jnp.exp2(lmid), jnp.exp2(lend)
        for pr in range(n_pairs):
            sl = slice(pr * LANES, (pr + 1) * LANES)
            ar.append(jnp.concatenate([a_til[:, sl], r_til[:, sl]], axis=0).astype(BF16))
            bk_chk.append(jnp.concatenate([_head_split(b_chk[:, sl].astype(BF16), lane_head),
                                           _head_split(k_chk[:, sl].astype(BF16), lane_head)], axis=0))
            bk_hat.append(jnp.concatenate([b_hat[:, sl], k_hat[:, sl]], axis=0).astype(BF16))
            vp.append(v[rows, sl])
            g_mid.append(gm[:, sl])
            g_end.append(ge[:, sl])
    v_split = [_head_split(t.astype(BF16), lane_head) for t in vp]
    state = [state_ref[n, pr] for n, pr in groups]

    g = [_mm_nt(ar[gi], bk_chk[gi]) for gi in range(len(groups))]
    x = [_mm_nt(ar[gi], state[gi] * g_mid[gi]) for gi in range(len(groups))]
    w_a = [jnp.where(strict2, gh[:c], 0.0) for gh in g]
    w_r = [jnp.where(incl2, gh[c:], 0.0).astype(BF16) for gh in g]
    ak_v = [_mm(w_a[gi][:, LANES:], v_split[gi]) for gi in range(len(groups))]
    t_inv = _tri_inverse_all([wa[:, :LANES] for wa in w_a], r_idx, c_idx, lane_head)
    u = [_mm(t_inv[gi], _head_split(x[gi][:c] + ak_v[gi], lane_head)) for gi in range(len(groups))]
    y_pair = [x[gi][c:] + _mm(w_r[gi], jnp.concatenate(
                  [_head_split(u[gi].astype(BF16), lane_head), v_split[gi]], axis=0))
              for gi in range(len(groups))]
    for gi, (n, pr) in enumerate(groups):
        uv = jnp.concatenate([u[gi], vp[gi]], axis=0)
        state_ref[n, pr] = (state[gi] * g_end[gi]
                            + jnp.where(same_head, _mm_tn(uv, bk_hat[gi]), 0.0))
    y = jnp.concatenate([jnp.concatenate(y_pair[n * n_pairs:(n + 1) * n_pairs], axis=1)
                         for n in range(nb)], axis=0)
    mean = _head_sums(y, hsum, 2) * (1.0 / HEAD_DIM)
    yc = y - mean
    var = _head_sums(yc * yc, hsum, 2) * (1.0 / HEAD_DIM)
    yn = yc * lax.rsqrt(var + GN_EPS) * ln_g + ln_b
    bonus = _head_sums(r * k * r_k, hsum, 1) * v
    out = ((yn + bonus) * gate).astype(o_ref.dtype)
    for n in range(nb):
        o_ref[n] = out[n * c:(n + 1) * c]


def _rwkv(rw3, mu, vec, wup, aup, gup, tril, hsum):
    b, s, cols = rw3.shape
    c = RWKV_CHUNK
    nb = RWKV_ROWS
    return pl.pallas_call(
        _rwkv_kernel,
        grid=(b // nb, s // c),
        in_specs=[pl.BlockSpec((nb, c, cols), lambda bb, i: (bb, i, 0)),
                  _const_spec(mu.shape), _const_spec(vec.shape), _const_spec(wup.shape),
                  _const_spec(aup.shape), _const_spec(gup.shape), _const_spec(tril.shape),
                  _const_spec(hsum.shape)],
        out_specs=pl.BlockSpec((nb, c, RWKV_W), lambda bb, i: (bb, i, 0)),
        out_shape=jax.ShapeDtypeStruct((b, s, RWKV_W), BF16),
        scratch_shapes=[pltpu.VMEM((nb, RWKV_W // LANES, LANES, LANES), F32),
                        pltpu.VMEM((nb, SUBLANES, cols), F32)],
        compiler_params=_compiler_params(2),
        name="rwkv7",
    )(rw3, mu, vec, wup, aup, gup, tril, hsum)


def _mlp_kernel(x_ref, att_ref, conv_ref, halo_ref, rw_ref, convw_ref, wo_att_ref, wo_conv_ref,
                wo_rw_ref, g_ref, wup_ref, wdn_ref, gfin_ref, o_ref, *, tiles_per_seq, final):
    tm = x_ref.shape[0]
    cw = CONV_W
    conv = conv_ref[...]
    zc = conv[:, cw:2 * cw] * conv[:, 2 * cw:]
    halo = halo_ref[...]
    seq_start = (pl.program_id(0) % tiles_per_seq) == 0
    zh = jnp.where(seq_start, 0.0, halo[:, cw:2 * cw] * halo[:, 2 * cw:])
    row = lax.broadcasted_iota(jnp.int32, (tm, 1), 0)
    z1 = jnp.where(row == 0, zh[SUBLANES - 1:SUBLANES], pltpu.roll(zc, 1, 0))
    z2 = jnp.where(row == 0, zh[SUBLANES - 2:SUBLANES - 1],
                   jnp.where(row == 1, zh[SUBLANES - 1:SUBLANES], pltpu.roll(zc, 2, 0)))
    cwt = convw_ref[...]
    o_conv = conv[:, :cw] * (cwt[0:1] * z2 + cwt[1:2] * z1 + cwt[2:3] * zc)

    x = x_ref[...]
    x = x + (jnp.dot(att_ref[...], wo_att_ref[...], preferred_element_type=F32)
             + jnp.dot(o_conv.astype(BF16), wo_conv_ref[...], preferred_element_type=F32)
             + jnp.dot(rw_ref[...], wo_rw_ref[...], preferred_element_type=F32))

    h = _rms(x, g_ref[...]).astype(BF16)
    d_ff = wup_ref.shape[1]
    acc = x
    for j in range(d_ff // FF_CHUNK):
        cols = slice(j * FF_CHUNK, (j + 1) * FF_CHUNK)
        up = jnp.maximum(jnp.dot(h, wup_ref[:, cols], preferred_element_type=F32), 0.0)
        acc = acc + jnp.dot((up * up).astype(BF16), wdn_ref[cols, :], preferred_element_type=F32)
    if final:
        acc = _rms(acc, gfin_ref[...])
    o_ref[...] = acc


def _mlp(x2, att2, conv2, rw2, convw, wo_att, wo_conv, wo_rw, g, wup, wdn, gfin, *, seq, final):
    t, d = x2.shape
    tm = ROW_TILE
    halo_blocks = tm // SUBLANES
    row = lambda n: pl.BlockSpec((tm, n), lambda i: (i, 0))
    halo = pl.BlockSpec((SUBLANES, conv2.shape[1]),
                        lambda i: (jnp.maximum(i * halo_blocks - 1, 0), 0))
    consts = [convw, wo_att, wo_conv, wo_rw, g, wup, wdn, gfin]
    return pl.pallas_call(
        functools.partial(_mlp_kernel, tiles_per_seq=seq // tm, final=final),
        grid=(t // tm,),
        in_specs=[row(d), row(att2.shape[1]), row(conv2.shape[1]), halo, row(rw2.shape[1])]
                 + [_const_spec(a.shape) for a in consts],
        out_specs=row(d),
        out_shape=jax.ShapeDtypeStruct((t, d), F32),
        compiler_params=_compiler_params(1),
        name="mix_mlp",
    )(x2, att2, conv2, conv2, rw2, *consts)


def kernel(x, norm_mix_g, w_in, lam_q1, lam_k1, lam_q2, lam_k2, subln_g, conv_w, shift_mu, rwkv_w0, rwkv_w_up, rwkv_a0, rwkv_a_up, rwkv_g_up, rwkv_k_k, rwkv_k_a, rwkv_r_k, lnx_g, lnx_b, w_out, norm_mlp_g, w_mlp_up, w_mlp_down, final_norm_g):
    bsz, seq, d = x.shape
    depth = w_in.shape[0]
    t = bsz * seq
    assert t % ROW_TILE == 0 and seq % ROW_TILE == 0 and seq % ATT_TILE == 0 and seq % RWKV_CHUNK == 0
    assert bsz % RWKV_ROWS == 0

    att_cols, conv_cols = 3 * ATT_W, 3 * CONV_W
    qscale = jnp.concatenate([jnp.full((1, ATT_W), DIFF_HALF ** -0.5 * math.log2(math.e), F32),
                              jnp.ones((1, ATT_W), F32)], axis=1)
    idx = jnp.arange(min(CUMSUM_BLOCK, RWKV_ROWS * RWKV_CHUNK))
    tril = jnp.logical_and(idx[None, :] <= idx[:, None],
                           idx[None, :] // RWKV_CHUNK == idx[:, None] // RWKV_CHUNK).astype(BF16)
    hidx = jnp.arange(LANES) // HEAD_DIM
    hsum = (hidx[:, None] == hidx[None, :]).astype(BF16)
    lora_pad = jnp.zeros((DECAY_LORA, RWKV_W), F32)

    x2 = x.reshape(t, d)
    for l in range(depth):
        lambda_init = 0.8 - 0.6 * math.exp(-0.3 * l)
        w_l = w_in[l].astype(BF16)
        qk2, vt, conv2, rw2 = _inproj(
            x2, norm_mix_g[l][None], qscale, w_l[:, :2 * ATT_W], w_l[:, 2 * ATT_W:att_cols].T,
            w_l[:, att_cols:att_cols + conv_cols], w_l[:, att_cols + conv_cols:])

        lam_params = jnp.stack([lam_q1[l], lam_k1[l], lam_q2[l], lam_k2[l]])
        o_att = _attention(qk2.reshape(bsz, seq, 2 * ATT_W), vt, lam_params, subln_g[l][:, None],
                           lambda_init)

        vec = jnp.stack([rwkv_w0[l], rwkv_a0[l], rwkv_k_k[l], rwkv_k_a[l], rwkv_r_k[l].reshape(-1),
                         lnx_g[l], lnx_b[l], jnp.zeros((RWKV_W,), F32)])
        wup = jnp.concatenate([rwkv_w_up[l], lora_pad], axis=0).astype(BF16)
        aup = jnp.concatenate([lora_pad, rwkv_a_up[l]], axis=0).astype(BF16)
        o_rw = _rwkv(rw2.reshape(bsz, seq, RWKV_COLS), shift_mu[l][None], vec, wup, aup,
                     rwkv_g_up[l].astype(BF16), tril, hsum)

        wo = w_out[l].astype(BF16)
        x2 = _mlp(x2, o_att.reshape(t, ATT_W), conv2, o_rw.reshape(t, RWKV_W), conv_w[l],
                  wo[:ATT_W], wo[ATT_W:ATT_W + CONV_W], wo[ATT_W + CONV_W:], norm_mlp_g[l][None],
                  w_mlp_up[l].astype(BF16), w_mlp_down[l].astype(BF16), final_norm_g[None],
                  seq=seq, final=(l == depth - 1))
    return x2.reshape(bsz, seq, d)
```

```python
import functools
import math

import jax
import jax.numpy as jnp
from jax import lax
from jax.experimental import pallas as pl
from jax.experimental.pallas import tpu as pltpu

F32 = jnp.float32
BF16 = jnp.bfloat16

HEAD_DIM = 64
DIFF_HALF = HEAD_DIM // 2
ATT_HEADS = 6
CONV_GROUPS = 4
RWKV_HEADS = 6
ATT_W = ATT_HEADS * HEAD_DIM
CONV_W = CONV_GROUPS * HEAD_DIM
RWKV_W = RWKV_HEADS * HEAD_DIM
CONV_K = 3
DECAY_LORA = 64
ICLR_LORA = 64
GATE_LORA = 128
RWKV_COLS = 3 * RWKV_W + DECAY_LORA + ICLR_LORA + GATE_LORA
NORM_EPS = 1e-6
GN_EPS = 64e-5

LANES = 128
SUBLANES = 8
HEADS_PER_VREG = LANES // HEAD_DIM
VMEM_LIMIT_BYTES = 56 * 1024 * 1024

ROW_TILE = 512
ATT_TILE = 1024
KEY_PARTS = 4
RWKV_CHUNK = 64
RWKV_ROWS = 8
CUMSUM_BLOCK = 256
FF_CHUNK = 1024
NEG_BIG = -1e30
ONES_ROWS = 16


def _compiler_params(n_axes):
    return pltpu.CompilerParams(
        dimension_semantics=("arbitrary",) * n_axes,
        vmem_limit_bytes=VMEM_LIMIT_BYTES,
    )


def _const_spec(shape):
    nd = len(shape)
    return pl.BlockSpec(shape, lambda *_: (0,) * nd)


def _mm(a, b):
    return jnp.dot(a.astype(BF16), b.astype(BF16), preferred_element_type=F32)


def _mm_nt(a, b):
    return lax.dot_general(a.astype(BF16), b.astype(BF16), (((1,), (1,)), ((), ())),
                           preferred_element_type=F32)


def _mm_tn(a, b):
    return lax.dot_general(a.astype(BF16), b.astype(BF16), (((0,), (0,)), ((), ())),
                           preferred_element_type=F32)


def _bf16_pieces(x, terms):
    pieces = []
    rem = x
    for t in range(terms):
        piece = rem.astype(BF16)
        pieces.append(piece)
        if t + 1 < terms:
            rem = rem - piece.astype(F32)
    return pieces


def _head_sums(x, ones_bf16, terms):
    out = []
    for s in range(x.shape[1] // LANES):
        xs = x[:, s * LANES:(s + 1) * LANES]
        out.append(functools.reduce(lambda a, b: a + b, [
            jnp.dot(piece, ones_bf16, preferred_element_type=F32) for piece in _bf16_pieces(xs, terms)]))
    return jnp.concatenate(out, axis=1)


def _cumsum_rows(tril_bf16, x):
    blk = tril_bf16.shape[0]
    out = []
    for s in range(x.shape[0] // blk):
        xs = x[s * blk:(s + 1) * blk]
        out.append(functools.reduce(lambda a, b: a + b, [
            jnp.dot(tril_bf16, piece, preferred_element_type=F32) for piece in _bf16_pieces(xs, 3)]))
    return jnp.concatenate(out, axis=0)


def _sigmoid(x):
    return 0.5 * jnp.tanh(0.5 * x) + 0.5


def _rms(x, g):
    ms = jnp.mean(x * x, axis=-1, keepdims=True)
    return x * lax.rsqrt(ms + NORM_EPS) * g


def _inproj_kernel(x_ref, g_ref, qscale_ref, w_qk_ref, w_vt_ref, w_conv_ref, w_rw_ref,
                   qk_ref, vt_ref, conv_ref, rw_ref):
    h = _rms(x_ref[...], g_ref[...]).astype(BF16)
    qk = jnp.dot(h, w_qk_ref[...], preferred_element_type=F32)
    qk_ref[...] = (qk * qscale_ref[...]).astype(BF16)
    vt_ref[...] = _mm_nt(w_vt_ref[...], h).astype(BF16)
    conv_ref[...] = jnp.dot(h, w_conv_ref[...], preferred_element_type=F32)
    rw_ref[...] = jnp.dot(h, w_rw_ref[...], preferred_element_type=F32)


def _inproj(x2, g, qscale, w_qk, w_vt, w_conv, w_rw):
    t, d = x2.shape
    tm = ROW_TILE
    row = lambda n: pl.BlockSpec((tm, n), lambda i: (i, 0))
    return pl.pallas_call(
        _inproj_kernel,
        grid=(t // tm,),
        in_specs=[row(d), _const_spec(g.shape), _const_spec(qscale.shape), _const_spec(w_qk.shape),
                  _const_spec(w_vt.shape), _const_spec(w_conv.shape), _const_spec(w_rw.shape)],
        out_specs=[row(2 * ATT_W), pl.BlockSpec((ATT_W, tm), lambda i: (0, i)),
                   row(3 * CONV_W), row(RWKV_COLS)],
        out_shape=[jax.ShapeDtypeStruct((t, 2 * ATT_W), BF16),
                   jax.ShapeDtypeStruct((ATT_W, t), BF16),
                   jax.ShapeDtypeStruct((t, 3 * CONV_W), F32),
                   jax.ShapeDtypeStruct((t, RWKV_COLS), F32)],
        compiler_params=_compiler_params(1),
        name="inproj",
    )(x2, g, qscale, w_qk, w_vt, w_conv, w_rw)


def _attn_kernel(lam_ref, sg_ref, q_ref, k_ref, vt_ref, o_ref, *, lambda_init):
    tq = q_ref.shape[1]
    i = pl.program_id(2)
    q = q_ref[0]
    lane = lax.broadcasted_iota(jnp.int32, (tq, LANES), 1)
    n_maps = LANES // DIFF_HALF
    maps = range(n_maps)
    q_sel = [jnp.where(lane // DIFF_HALF == c, q, jnp.zeros_like(q)) for c in maps]

    def scores(k_start, tk, q_lo=0, k_off=None):
        ks = k_ref[0, pl.ds(k_start, tk), :]
        out = [_mm_nt(ks, q_sel[c][q_lo:]) for c in maps]
        if k_off is not None:
            key_pos = lax.broadcasted_iota(jnp.int32, (tk, tq - q_lo), 0) + k_off
            qry_pos = lax.broadcasted_iota(jnp.int32, (tk, tq - q_lo), 1) + q_lo
            out = [jnp.where(key_pos <= qry_pos, s, -jnp.inf) for s in out]
        return out

    def softmax(s, m):
        m_new = [jnp.maximum(m[c], jnp.max(s[c], axis=0, keepdims=True)) for c in maps]
        alpha = [jnp.exp2(m[c] - m_new[c]) for c in maps]
        p = [jnp.exp2(s[c] - m_new[c]).astype(BF16) for c in maps]
        return m_new, alpha, p

    def values(k_start, tk, p):
        vt = vt_ref[:, pl.ds(k_start, tk)]
        ones = jnp.ones((ONES_ROWS, tk), BF16)
        vt_aug = [jnp.concatenate([vt[hh * HEAD_DIM:(hh + 1) * HEAD_DIM], ones], axis=0)
                  for hh in range(HEADS_PER_VREG)]
        return [jnp.dot(vt_aug[c // 2], p[c], preferred_element_type=F32) for c in maps]

    def fold(k_start, tk, carry, q_lo=0, k_off=None, parts=1):
        sub = tk // parts
        s_parts = [scores(k_start + n * sub, sub, q_lo, None if k_off is None else k_off + n * sub)
                   for n in range(parts)]
        m, l, acc = carry
        m_new = [m[c][:, q_lo:] for c in maps]
        l_new = [l[c][:, q_lo:] for c in maps]
        acc_new = [acc[c][:, q_lo:] for c in maps]
        for n in range(parts):
            m_new, alpha, p = softmax(s_parts[n], m_new)
            pv = values(k_start + n * sub, sub, p)
            l_new = [alpha[c] * l_new[c] + pv[c][HEAD_DIM:HEAD_DIM + 1] for c in maps]
            acc_new = [alpha[c] * acc_new[c] + pv[c][:HEAD_DIM] for c in maps]
        if q_lo:
            keep = lambda old, new: [jnp.concatenate([old[c][:, :q_lo], new[c]], axis=1) for c in maps]
            m_new, l_new, acc_new = keep(m, m_new), keep(l, l_new), keep(acc, acc_new)
        return m_new, l_new, acc_new

    carry = ([jnp.full((1, tq), NEG_BIG, F32) for _ in maps], [jnp.zeros((1, tq), F32) for _ in maps],
             [jnp.zeros((HEAD_DIM, tq), F32) for _ in maps])
    carry = lax.fori_loop(
        0, i, lambda j, cr: fold(pl.multiple_of(j * tq, tq), tq, cr, parts=KEY_PARTS), carry)
    half = tq // 2
    diag = pl.multiple_of(i * tq, tq)
    carry = fold(diag, half, carry, q_lo=0, k_off=0, parts=KEY_PARTS // 2)
    _, l, acc = fold(diag + half, half, carry, q_lo=half, k_off=half, parts=KEY_PARTS // 2)

    lp = lam_ref[...]
    lam = (jnp.exp(jnp.sum(lp[0:1] * lp[1:2], axis=-1, keepdims=True))
           - jnp.exp(jnp.sum(lp[2:3] * lp[3:4], axis=-1, keepdims=True)) + lambda_init)
    heads = []
    for hh in range(HEADS_PER_VREG):
        o = acc[2 * hh] * (1.0 / l[2 * hh]) - lam * (acc[2 * hh + 1] * (1.0 / l[2 * hh + 1]))
        ms = jnp.mean(o * o, axis=0, keepdims=True)
        heads.append(o * lax.rsqrt(ms + NORM_EPS) * sg_ref[...] * (1.0 - lambda_init))
    o_ref[0] = jnp.concatenate(heads, axis=0).T.astype(o_ref.dtype)


def _attention(qk3, vt, lam_params, subln_col, lambda_init):
    b, s, _ = qk3.shape
    tq = ATT_TILE
    n_pairs = ATT_W // LANES
    return pl.pallas_call(
        functools.partial(_attn_kernel, lambda_init=lambda_init),
        grid=(b, n_pairs, s // tq),
        in_specs=[_const_spec(lam_params.shape), _const_spec(subln_col.shape),
                  pl.BlockSpec((1, tq, LANES), lambda bb, hp, i: (bb, i, hp)),
                  pl.BlockSpec((1, s, LANES), lambda bb, hp, i: (bb, 0, n_pairs + hp)),
                  pl.BlockSpec((LANES, s), lambda bb, hp, i: (hp, bb))],
        out_specs=pl.BlockSpec((1, tq, LANES), lambda bb, hp, i: (bb, i, hp)),
        out_shape=jax.ShapeDtypeStruct((b, s, ATT_W), BF16),
        compiler_params=_compiler_params(3),
        name="diff_attn",
    )(lam_params, subln_col, qk3, qk3, vt)


def _head_split(x, lane_head):
    return jnp.concatenate([jnp.where(lane_head == hh, x, jnp.zeros_like(x))
                            for hh in range(HEADS_PER_VREG)], axis=0)


def _tri_inverse_all(mats, r_idx, c_idx, lane_head):
    c = mats[0].shape[0]
    base = 16
    same = lambda blk: (r_idx // blk) == (c_idx // blk)
    eye = (r_idx == c_idx).astype(F32)
    mul = lambda ls, rs: [_mm(l, _head_split(r, lane_head)) for l, r in zip(ls, rs)]
    in_base = same(base)
    powers = [jnp.where(in_base, a, 0.0) for a in mats]
    invs = [eye + d for d in powers]
    span = 2
    while span < base:
        powers = mul(powers, powers)
        invs = [inv + t for inv, t in zip(invs, mul(powers, invs))]
        span *= 2
    blk = base
    while blk < c:
        off_diag = jnp.logical_and(same(2 * blk), jnp.logical_not(same(blk)))
        tmp = mul([jnp.where(off_diag, a, 0.0) for a in mats], invs)
        invs = [inv + t for inv, t in zip(invs, mul(invs, tmp))]
        blk *= 2
    return invs


def _rwkv_kernel(p_ref, mu_ref, vec_ref, wup_ref, aup_ref, gup_ref, tril_ref, hsum_ref,
                 o_ref, state_ref, prev_ref):
    nb, c, _ = p_ref.shape
    w = RWKV_W
    assert HEADS_PER_VREG * c == LANES

    @pl.when(pl.program_id(1) == 0)
    def _():
        state_ref[...] = jnp.zeros_like(state_ref)
        prev_ref[...] = jnp.zeros_like(prev_ref)

    p = jnp.concatenate([p_ref[n] for n in range(nb)], axis=0)
    row1 = lax.broadcasted_iota(jnp.int32, (nb * c, 1), 0)
    prev = pltpu.roll(p, 1, 0)
    for n in range(nb):
        prev = jnp.where(row1 == n * c, prev_ref[n, SUBLANES - 1:SUBLANES, :], prev)
        prev_ref[n] = p[(n + 1) * c - SUBLANES:(n + 1) * c, :]
    z = p + mu_ref[...] * (prev - p)

    vec = vec_ref[...]
    w0, a0, k_k, k_a, r_k, ln_g, ln_b = (vec[n:n + 1] for n in range(7))
    r = z[:, 0:w]
    k = z[:, w:2 * w]
    v = z[:, 2 * w:3 * w]
    lora_in = z[:, 3 * w:3 * w + DECAY_LORA + ICLR_LORA]
    g_in = z[:, 3 * w + DECAY_LORA + ICLR_LORA:]

    wdec = w0 + _mm(jnp.tanh(lora_in), wup_ref[...])
    logw = -(math.exp(-0.5) * math.log2(math.e)) * _sigmoid(wdec)
    iclr = _sigmoid(a0 + _mm(lora_in, aup_ref[...]))
    gate = _mm(_sigmoid(g_in), gup_ref[...])

    hsum = hsum_ref[...]
    kk = k * k_k
    kk = kk * lax.rsqrt(jnp.maximum(_head_sums(kk * kk, hsum, 1), 1e-24))
    k = k * (1.0 + (iclr - 1.0) * k_a)
    a_vec = -kk
    b_vec = kk * iclr
    lcum_all = _cumsum_rows(tril_ref[...], logw)

    r_idx = lax.broadcasted_iota(jnp.int32, (c, LANES), 0)
    lane = lax.broadcasted_iota(jnp.int32, (c, LANES), 1)
    c_idx = lane % c
    lane_head = lane // HEAD_DIM
    strict = c_idx < r_idx
    incl = c_idx <= r_idx
    strict2 = jnp.concatenate([strict, strict], axis=1)
    incl2 = jnp.concatenate([incl, incl], axis=1)
    blk_r = lax.broadcasted_iota(jnp.int32, (LANES, LANES), 0) // HEAD_DIM
    blk_c = lax.broadcasted_iota(jnp.int32, (LANES, LANES), 1) // HEAD_DIM
    same_head = blk_r == blk_c

    n_pairs = w // LANES
    groups = [(n, pr) for n in range(nb) for pr in range(n_pairs)]
    ar, bk_chk, bk_hat, vp, g_mid, g_end = [], [], [], [], [], []
    for n in range(nb):
        rows = slice(n * c, (n + 1) * c)
        lcum = lcum_all[rows]
        lmid = lcum[c // 2 - 1:c // 2, :]
        lend = lcum[c - 1:c, :]
        lc = lcum - lmid
        g_inv = jnp.exp2(-lc)
        g_tail = jnp.exp2(lend - lcum)
        a_til = a_vec[rows] * jnp.exp2(lc - logw[rows])
        r_til = r[rows] * jnp.exp2(lc)
        b_chk = b_vec[rows] * g_inv
        k_chk = k[rows] * g_inv
        b_hat = b_vec[rows] * g_tail
        k_hat = k[rows] * g_tail
        gm, ge = jnp.exp2(lmid), jnp.exp2(lend)
        for pr in range(n_pairs):
            sl = slice(pr * LANES, (pr + 1) * LANES)
            ar.append(jnp.concatenate([a_til[:, sl], r_til[:, sl]], axis=0).astype(BF16))
            bk_chk.append(jnp.concatenate([_head_split(b_chk[:, sl].astype(BF16), lane_head),
                                           _head_split(k_chk[:, sl].astype(BF16), lane_head)], axis=0))
            bk_hat.append(jnp.concatenate([b_hat[:, sl], k_hat[:, sl]], axis=0).astype(BF16))
            vp.append(v[rows, sl])
            g_mid.append(gm[:, sl])
            g_end.append(ge[:, sl])
    v_split = [_head_split(t.astype(BF16), lane_head) for t in vp]
    state = [state_ref[n, pr] for n, pr in groups]

    g = [_mm_nt(ar[gi], bk_chk[gi]) for gi in range(len(groups))]
    x = [_mm_nt(ar[gi], state[gi] * g_mid[gi]) for gi in range(len(groups))]
    w_a = [jnp.where(strict2, gh[:c], 0.0) for gh in g]
    w_r = [jnp.where(incl2, gh[c:], 0.0).astype(BF16) for gh in g]
    ak_v = [_mm(w_a[gi][:, LANES:], v_split[gi]) for gi in range(len(groups))]
    t_inv = _tri_inverse_all([wa[:, :LANES] for wa in w_a], r_idx, c_idx, lane_head)
    u = [_mm(t_inv[gi], _head_split(x[gi][:c] + ak_v[gi], lane_head)) for gi in range(len(groups))]
    y_pair = [x[gi][c:] + _mm(w_r[gi], jnp.concatenate(
                  [_head_split(u[gi].astype(BF16), lane_head), v_split[gi]], axis=0))
              for gi in range(len(groups))]
    for gi, (n, pr) in enumerate(groups):
        uv = jnp.concatenate([u[gi], vp[gi]], axis=0)
        state_ref[n, pr] = (state[gi] * g_end[gi]
                            + jnp.where(same_head, _mm_tn(uv, bk_hat[gi]), 0.0))
    y = jnp.concatenate([jnp.concatenate(y_pair[n * n_pairs:(n + 1) * n_pairs], axis=1)
                         for n in range(nb)], axis=0)
    mean = _head_sums(y, hsum, 2) * (1.0 / HEAD_DIM)
    yc = y - mean
    var = _head_sums(yc * yc, hsum, 2) * (1.0 / HEAD_DIM)
    yn = yc * lax.rsqrt(var + GN_EPS) * ln_g + ln_b
    bonus = _head_sums(r * k * r_k, hsum, 1) * v
    out = ((yn + bonus) * gate).astype(o_ref.dtype)
    for n in range(nb):
        o_ref[n] = out[n * c:(n + 1) * c]


def _rwkv(rw3, mu, vec, wup, aup, gup, tril, hsum):
    b, s, cols = rw3.shape
    c = RWKV_CHUNK
    nb = RWKV_ROWS
    return pl.pallas_call(
        _rwkv_kernel,
        grid=(b // nb, s // c),
        in_specs=[pl.BlockSpec((nb, c, cols), lambda bb, i: (bb, i, 0)),
                  _const_spec(mu.shape), _const_spec(vec.shape), _const_spec(wup.shape),
                  _const_spec(aup.shape), _const_spec(gup.shape), _const_spec(tril.shape),
                  _const_spec(hsum.shape)],
        out_specs=pl.BlockSpec((nb, c, RWKV_W), lambda bb, i: (bb, i, 0)),
        out_shape=jax.ShapeDtypeStruct((b, s, RWKV_W), BF16),
        scratch_shapes=[pltpu.VMEM((nb, RWKV_W // LANES, LANES, LANES), F32),
                        pltpu.VMEM((nb, SUBLANES, cols), F32)],
        compiler_params=_compiler_params(2),
        name="rwkv7",
    )(rw3, mu, vec, wup, aup, gup, tril, hsum)


def _mlp_kernel(x_ref, att_ref, conv_ref, halo_ref, rw_ref, convw_ref, wo_att_ref, wo_conv_ref,
                wo_rw_ref, g_ref, wup_ref, wdn_ref, gfin_ref, o_ref, *, tiles_per_seq, final):
    tm = x_ref.shape[0]
    cw = CONV_W
    conv = conv_ref[...]
    zc = conv[:, cw:2 * cw] * conv[:, 2 * cw:]
    halo = halo_ref[...]
    seq_start = (pl.program_id(0) % tiles_per_seq) == 0
    zh = jnp.where(seq_start, 0.0, halo[:, cw:2 * cw] * halo[:, 2 * cw:])
    row = lax.broadcasted_iota(jnp.int32, (tm, 1), 0)
    z1 = jnp.where(row == 0, zh[SUBLANES - 1:SUBLANES], pltpu.roll(zc, 1, 0))
    z2 = jnp.where(row == 0, zh[SUBLANES - 2:SUBLANES - 1],
                   jnp.where(row == 1, zh[SUBLANES - 1:SUBLANES], pltpu.roll(zc, 2, 0)))
    cwt = convw_ref[...]
    o_conv = conv[:, :cw] * (cwt[0:1] * z2 + cwt[1:2] * z1 + cwt[2:3] * zc)

    x = x_ref[...]
    x = x + (jnp.dot(att_ref[...], wo_att_ref[...], preferred_element_type=F32)
             + jnp.dot(o_conv.astype(BF16), wo_conv_ref[...], preferred_element_type=F32)
             + jnp.dot(rw_ref[...], wo_rw_ref[...], preferred_element_type=F32))

    h = _rms(x, g_ref[...]).astype(BF16)
    d_ff = wup_ref.shape[1]
    acc = x
    for j in range(d_ff // FF_CHUNK):
        cols = slice(j * FF_CHUNK, (j + 1) * FF_CHUNK)
        up = jnp.maximum(jnp.dot(h, wup_ref[:, cols], preferred_element_type=F32), 0.0)
        acc = acc + jnp.dot((up * up).astype(BF16), wdn_ref[cols, :], preferred_element_type=F32)
    if final:
        acc = _rms(acc, gfin_ref[...])
    o_ref[...] = acc


def _mlp(x2, att2, conv2, rw2, convw, wo_att, wo_conv, wo_rw, g, wup, wdn, gfin, *, seq, final):
    t, d = x2.shape
    tm = ROW_TILE
    halo_blocks = tm // SUBLANES
    row = lambda n: pl.BlockSpec((tm, n), lambda i: (i, 0))
    halo = pl.BlockSpec((SUBLANES, conv2.shape[1]),
                        lambda i: (jnp.maximum(i * halo_blocks - 1, 0), 0))
    consts = [convw, wo_att, wo_conv, wo_rw, g, wup, wdn, gfin]
    return pl.pallas_call(
        functools.partial(_mlp_kernel, tiles_per_seq=seq // tm, final=final),
        grid=(t // tm,),
        in_specs=[row(d), row(att2.shape[1]), row(conv2.shape[1]), halo, row(rw2.shape[1])]
                 + [_const_spec(a.shape) for a in consts],
        out_specs=row(d),
        out_shape=jax.ShapeDtypeStruct((t, d), F32),
        compiler_params=_compiler_params(1),
        name="mix_mlp",
    )(x2, att2, conv2, conv2, rw2, *consts)


def kernel(x, norm_mix_g, w_in, lam_q1, lam_k1, lam_q2, lam_k2, subln_g, conv_w, shift_mu, rwkv_w0, rwkv_w_up, rwkv_a0, rwkv_a_up, rwkv_g_up, rwkv_k_k, rwkv_k_a, rwkv_r_k, lnx_g, lnx_b, w_out, norm_mlp_g, w_mlp_up, w_mlp_down, final_norm_g):
    bsz, seq, d = x.shape
    depth = w_in.shape[0]
    t = bsz * seq
    assert t % ROW_TILE == 0 and seq % ROW_TILE == 0 and seq % ATT_TILE == 0 and seq % RWKV_CHUNK == 0
    assert bsz % RWKV_ROWS == 0

    att_cols, conv_cols = 3 * ATT_W, 3 * CONV_W
    qscale = jnp.concatenate([jnp.full((1, ATT_W), DIFF_HALF ** -0.5 * math.log2(math.e), F32),
                              jnp.ones((1, ATT_W), F32)], axis=1)
    idx = jnp.arange(min(CUMSUM_BLOCK, RWKV_ROWS * RWKV_CHUNK))
    tril = jnp.logical_and(idx[None, :] <= idx[:, None],
                           idx[None, :] // RWKV_CHUNK == idx[:, None] // RWKV_CHUNK).astype(BF16)
    hidx = jnp.arange(LANES) // HEAD_DIM
    hsum = (hidx[:, None] == hidx[None, :]).astype(BF16)
    lora_pad = jnp.zeros((DECAY_LORA, RWKV_W), F32)

    x2 = x.reshape(t, d)
    for l in range(depth):
        lambda_init = 0.8 - 0.6 * math.exp(-0.3 * l)
        w_l = w_in[l].astype(BF16)
        qk2, vt, conv2, rw2 = _inproj(
            x2, norm_mix_g[l][None], qscale, w_l[:, :2 * ATT_W], w_l[:, 2 * ATT_W:att_cols].T,
            w_l[:, att_cols:att_cols + conv_cols], w_l[:, att_cols + conv_cols:])

        lam_params = jnp.stack([lam_q1[l], lam_k1[l], lam_q2[l], lam_k2[l]])
        o_att = _attention(qk2.reshape(bsz, seq, 2 * ATT_W), vt, lam_params, subln_g[l][:, None],
                           lambda_init)

        vec = jnp.stack([rwkv_w0[l], rwkv_a0[l], rwkv_k_k[l], rwkv_k_a[l], rwkv_r_k[l].reshape(-1),
                         lnx_g[l], lnx_b[l], jnp.zeros((RWKV_W,), F32)])
        wup = jnp.concatenate([rwkv_w_up[l], lora_pad], axis=0).astype(BF16)
        aup = jnp.concatenate([lora_pad, rwkv_a_up[l]], axis=0).astype(BF16)
        o_rw = _rwkv(rw2.reshape(bsz, seq, RWKV_COLS), shift_mu[l][None], vec, wup, aup,
                     rwkv_g_up[l].astype(BF16), tril, hsum)

        wo = w_out[l].astype(BF16)
        x2 = _mlp(x2, o_att.reshape(t, ATT_W), conv2, o_rw.reshape(t, RWKV_W), conv_w[l],
                  wo[:ATT_W], wo[ATT_W:ATT_W + CONV_W], wo[ATT_W + CONV_W:], norm_mlp_g[l][None],
                  w_mlp_up[l].astype(BF16), w_mlp_down[l].astype(BF16), final_norm_g[None],
                  seq=seq, final=(l == depth - 1))
    return x2.reshape(bsz, seq, d)
```

```python
import functools
import math

import jax
import jax.numpy as jnp
from jax import lax
from jax.experimental import pallas as pl
from jax.experimental.pallas import tpu as pltpu

F32 = jnp.float32
BF16 = jnp.bfloat16

HEAD_DIM = 64
DIFF_HALF = HEAD_DIM // 2
ATT_HEADS = 6
CONV_GROUPS = 4
RWKV_HEADS = 6
ATT_W = ATT_HEADS * HEAD_DIM
CONV_W = CONV_GROUPS * HEAD_DIM
RWKV_W = RWKV_HEADS * HEAD_DIM
CONV_K = 3
DECAY_LORA = 64
ICLR_LORA = 64
GATE_LORA = 128
RWKV_COLS = 3 * RWKV_W + DECAY_LORA + ICLR_LORA + GATE_LORA
NORM_EPS = 1e-6
GN_EPS = 64e-5

LANES = 128
SUBLANES = 8
HEADS_PER_VREG = LANES // HEAD_DIM
VMEM_LIMIT_BYTES = 56 * 1024 * 1024

ROW_TILE = 512
ATT_TILE = 1024
KEY_PARTS = 4
RWKV_CHUNK = 64
RWKV_ROWS = 8
CUMSUM_BLOCK = 256
FF_CHUNK = 1024
NEG_BIG = -1e30
ONES_ROWS = 16


def _compiler_params(n_axes):
    return pltpu.CompilerParams(
        dimension_semantics=("arbitrary",) * n_axes,
        vmem_limit_bytes=VMEM_LIMIT_BYTES,
    )


def _const_spec(shape):
    nd = len(shape)
    return pl.BlockSpec(shape, lambda *_: (0,) * nd)


def _mm(a, b):
    return jnp.dot(a.astype(BF16), b.astype(BF16), preferred_element_type=F32)


def _mm_nt(a, b):
    return lax.dot_general(a.astype(BF16), b.astype(BF16), (((1,), (1,)), ((), ())),
                           preferred_element_type=F32)


def _mm_tn(a, b):
    return lax.dot_general(a.astype(BF16), b.astype(BF16), (((0,), (0,)), ((), ())),
                           preferred_element_type=F32)


def _bf16_pieces(x, terms):
    pieces = []
    rem = x
    for t in range(terms):
        piece = rem.astype(BF16)
        pieces.append(piece)
        if t + 1 < terms:
            rem = rem - piece.astype(F32)
    return pieces


def _head_sums(x, ones_bf16, terms):
    out = []
    for s in range(x.shape[1] // LANES):
        xs = x[:, s * LANES:(s + 1) * LANES]
        out.append(functools.reduce(lambda a, b: a + b, [
            jnp.dot(piece, ones_bf16, preferred_element_type=F32) for piece in _bf16_pieces(xs, terms)]))
    return jnp.concatenate(out, axis=1)


def _cumsum_rows(tril_bf16, x):
    blk = tril_bf16.shape[0]
    out = []
    for s in range(x.shape[0] // blk):
        xs = x[s * blk:(s + 1) * blk]
        out.append(functools.reduce(lambda a, b: a + b, [
            jnp.dot(tril_bf16, piece, preferred_element_type=F32) for piece in _bf16_pieces(xs, 3)]))
    return jnp.concatenate(out, axis=0)


def _sigmoid(x):
    return 0.5 * jnp.tanh(0.5 * x) + 0.5


def _rms(x, g):
    ms = jnp.mean(x * x, axis=-1, keepdims=True)
    return x * lax.rsqrt(ms + NORM_EPS) * g


def _inproj_kernel(x_ref, g_ref, qscale_ref, w_qk_ref, w_vt_ref, w_conv_ref, w_rw_ref,
                   qk_ref, vt_ref, conv_ref, rw_ref):
    h = _rms(x_ref[...], g_ref[...]).astype(BF16)
    qk = jnp.dot(h, w_qk_ref[...], preferred_element_type=F32)
    qk_ref[...] = (qk * qscale_ref[...]).astype(BF16)
    vt_ref[...] = _mm_nt(w_vt_ref[...], h).astype(BF16)
    conv_ref[...] = jnp.dot(h, w_conv_ref[...], preferred_element_type=F32)
    rw_ref[...] = jnp.dot(h, w_rw_ref[...], preferred_element_type=F32)


def _inproj(x2, g, qscale, w_qk, w_vt, w_conv, w_rw):
    t, d = x2.shape
    tm = ROW_TILE
    row = lambda n: pl.BlockSpec((tm, n), lambda i: (i, 0))
    return pl.pallas_call(
        _inproj_kernel,
        grid=(t // tm,),
        in_specs=[row(d), _const_spec(g.shape), _const_spec(qscale.shape), _const_spec(w_qk.shape),
                  _const_spec(w_vt.shape), _const_spec(w_conv.shape), _const_spec(w_rw.shape)],
        out_specs=[row(2 * ATT_W), pl.BlockSpec((ATT_W, tm), lambda i: (0, i)),
                   row(3 * CONV_W), row(RWKV_COLS)],
        out_shape=[jax.ShapeDtypeStruct((t, 2 * ATT_W), BF16),
                   jax.ShapeDtypeStruct((ATT_W, t), BF16),
                   jax.ShapeDtypeStruct((t, 3 * CONV_W), F32),
                   jax.ShapeDtypeStruct((t, RWKV_COLS), F32)],
        compiler_params=_compiler_params(1),
        name="inproj",
    )(x2, g, qscale, w_qk, w_vt, w_conv, w_rw)


def _attn_kernel(lam_ref, sg_ref, q_ref, k_ref, vt_ref, o_ref, *, lambda_init):
    tq = q_ref.shape[1]
    i = pl.program_id(2)
    q = q_ref[0]
    lane = lax.broadcasted_iota(jnp.int32, (tq, LANES), 1)
    n_maps = LANES // DIFF_HALF
    maps = range(n_maps)
    q_sel = [jnp.where(lane // DIFF_HALF == c, q, jnp.zeros_like(q)) for c in maps]

    def scores(k_start, tk, q_lo=0, k_off=None):
        ks = k_ref[0, pl.ds(k_start, tk), :]
        out = [_mm_nt(ks, q_sel[c][q_lo:]) for c in maps]
        if k_off is not None:
            key_pos = lax.broadcasted_iota(jnp.int32, (tk, tq - q_lo), 0) + k_off
            qry_pos = lax.broadcasted_iota(jnp.int32, (tk, tq - q_lo), 1) + q_lo
            out = [jnp.where(key_pos <= qry_pos, s, -jnp.inf) for s in out]
        return out

    def softmax(s, m):
        m_new = [jnp.maximum(m[c], jnp.max(s[c], axis=0, keepdims=True)) for c in maps]
        alpha = [jnp.exp2(m[c] - m_new[c]) for c in maps]
        p = [jnp.exp2(s[c] - m_new[c]).astype(BF16) for c in maps]
        return m_new, alpha, p

    def values(k_start, tk, p):
        vt = vt_ref[:, pl.ds(k_start, tk)]
        ones = jnp.ones((ONES_ROWS, tk), BF16)
        vt_aug = [jnp.concatenate([vt[hh * HEAD_DIM:(hh + 1) * HEAD_DIM], ones], axis=0)
                  for hh in range(HEADS_PER_VREG)]
        return [jnp.dot(vt_aug[c // 2], p[c], preferred_element_type=F32) for c in maps]

    def fold(k_start, tk, carry, q_lo=0, k_off=None, parts=1):
        sub = tk // parts
        s_parts = [scores(k_start + n * sub, sub, q_lo, None if k_off is None else k_off + n * sub)
                   for n in range(parts)]
        m, l, acc = carry
        m_new = [m[c][:, q_lo:] for c in maps]
        l_new = [l[c][:, q_lo:] for c in maps]
        acc_new = [acc[c][:, q_lo:] for c in maps]
        for n in range(parts):
            m_new, alpha, p = softmax(s_parts[n], m_new)
            pv = values(k_start + n * sub, sub, p)
            l_new = [alpha[c] * l_new[c] + pv[c][HEAD_DIM:HEAD_DIM + 1] for c in maps]
            acc_new = [alpha[c] * acc_new[c] + pv[c][:HEAD_DIM] for c in maps]
        if q_lo:
            keep = lambda old, new: [jnp.concatenate([old[c][:, :q_lo], new[c]], axis=1) for c in maps]
            m_new, l_new, acc_new = keep(m, m_new), keep(l, l_new), keep(acc, acc_new)
        return m_new, l_new, acc_new

    carry = ([jnp.full((1, tq), NEG_BIG, F32) for _ in maps], [jnp.zeros((1, tq), F32) for _ in maps],
             [jnp.zeros((HEAD_DIM, tq), F32) for _ in maps])
    carry = lax.fori_loop(
        0, i, lambda j, cr: fold(pl.multiple_of(j * tq, tq), tq, cr, parts=KEY_PARTS), carry)
    half = tq // 2
    diag = pl.multiple_of(i * tq, tq)
    carry = fold(diag, half, carry, q_lo=0, k_off=0, parts=KEY_PARTS // 2)
    _, l, acc = fold(diag + half, half, carry, q_lo=half, k_off=half, parts=KEY_PARTS // 2)

    lp = lam_ref[...]
    lam = (jnp.exp(jnp.sum(lp[0:1] * lp[1:2], axis=-1, keepdims=True))
           - jnp.exp(jnp.sum(lp[2:3] * lp[3:4], axis=-1, keepdims=True)) + lambda_init)
    heads = []
    for hh in range(HEADS_PER_VREG):
        o = acc[2 * hh] * (1.0 / l[2 * hh]) - lam * (acc[2 * hh + 1] * (1.0 / l[2 * hh + 1]))
        ms = jnp.mean(o * o, axis=0, keepdims=True)
        heads.append(o * lax.rsqrt(ms + NORM_EPS) * sg_ref[...] * (1.0 - lambda_init))
    o_ref[0] = jnp.concatenate(heads, axis=0).T.astype(o_ref.dtype)


def _attention(qk3, vt, lam_params, subln_col, lambda_init):
    b, s, _ = qk3.shape
    tq = ATT_TILE
    n_pairs = ATT_W // LANES
    return pl.pallas_call(
        functools.partial(_attn_kernel, lambda_init=lambda_init),
        grid=(b, n_pairs, s // tq),
        in_specs=[_const_spec(lam_params.shape), _const_spec(subln_col.shape),
                  pl.BlockSpec((1, tq, LANES), lambda bb, hp, i: (bb, i, hp)),
                  pl.BlockSpec((1, s, LANES), lambda bb, hp, i: (bb, 0, n_pairs + hp)),
                  pl.BlockSpec((LANES, s), lambda bb, hp, i: (hp, bb))],
        out_specs=pl.BlockSpec((1, tq, LANES), lambda bb, hp, i: (bb, i, hp)),
        out_shape=jax.ShapeDtypeStruct((b, s, ATT_W), BF16),
        compiler_params=_compiler_params(3),
        name="diff_attn",
    )(lam_params, subln_col, qk3, qk3, vt)


def _head_split(x, lane_head):
    return jnp.concatenate([jnp.where(lane_head == hh, x, jnp.zeros_like(x))
                            for hh in range(HEADS_PER_VREG)], axis=0)


def _tri_inverse_all(mats, r_idx, c_idx, lane_head):
    c = mats[0].shape[0]
    base = 16
    same = lambda blk: (r_idx // blk) == (c_idx // blk)
    eye = (r_idx == c_idx).astype(F32)
    mul = lambda ls, rs: [_mm(l, _head_split(r, lane_head)) for l, r in zip(ls, rs)]
    in_base = same(base)
    diag = [jnp.where(in_base, a, 0.0) for a in mats]
    powers = mul(diag, diag)
    invs = [eye + d for d in diag]
    span = 2
    while 2 * span < base:
        both = mul([jnp.concatenate([pw, inv], axis=0) for pw, inv in zip(powers, invs)], powers)
        powers = [t[:c] for t in both]
        invs = [inv + t[c:] for inv, t in zip(invs, both)]
        span *= 2
    invs = [inv + t for inv, t in zip(invs, mul(invs, powers))]
    blk = base
    while blk < c:
        off_diag = jnp.logical_and(same(2 * blk), jnp.logical_not(same(blk)))
        tmp = mul([jnp.where(off_diag, a, 0.0) for a in mats], invs)
        invs = [inv + t for inv, t in zip(invs, mul(invs, tmp))]
        blk *= 2
    return invs


def _rwkv_kernel(p_ref, mu_ref, vec_ref, wup_ref, aup_ref, gup_ref, tril_ref, hsum_ref,
                 o_ref, state_ref, prev_ref):
    nb, c, _ = p_ref.shape
    w = RWKV_W
    assert HEADS_PER_VREG * c == LANES

    @pl.when(pl.program_id(1) == 0)
    def _():
        state_ref[...] = jnp.zeros_like(state_ref)
        prev_ref[...] = jnp.zeros_like(prev_ref)

    p = jnp.concatenate([p_ref[n] for n in range(nb)], axis=0)
    row1 = lax.broadcasted_iota(jnp.int32, (nb * c, 1), 0)
    prev = pltpu.roll(p, 1, 0)
    for n in range(nb):
        prev = jnp.where(row1 == n * c, prev_ref[n, SUBLANES - 1:SUBLANES, :], prev)
        prev_ref[n] = p[(n + 1) * c - SUBLANES:(n + 1) * c, :]
    z = p + mu_ref[...] * (prev - p)

    vec = vec_ref[...]
    w0, a0, k_k, k_a, r_k, ln_g, ln_b = (vec[n:n + 1] for n in range(7))
    r = z[:, 0:w]
    k = z[:, w:2 * w]
    v = z[:, 2 * w:3 * w]
    lora_in = z[:, 3 * w:3 * w + DECAY_LORA + ICLR_LORA]
    g_in = z[:, 3 * w + DECAY_LORA + ICLR_LORA:]

    wdec = w0 + _mm(jnp.tanh(lora_in), wup_ref[...])
    logw = -(math.exp(-0.5) * math.log2(math.e)) * _sigmoid(wdec)
    iclr = _sigmoid(a0 + _mm(lora_in, aup_ref[...]))
    gate = _mm(_sigmoid(g_in), gup_ref[...])

    hsum = hsum_ref[...]
    kk = k * k_k
    kk = kk * lax.rsqrt(jnp.maximum(_head_sums(kk * kk, hsum, 1), 1e-24))
    k = k * (1.0 + (iclr - 1.0) * k_a)
    a_vec = -kk
    b_vec = kk * iclr
    lcum_all = _cumsum_rows(tril_ref[...], logw)

    r_idx = lax.broadcasted_iota(jnp.int32, (c, LANES), 0)
    lane = lax.broadcasted_iota(jnp.int32, (c, LANES), 1)
    c_idx = lane % c
    lane_head = lane // HEAD_DIM
    strict = c_idx < r_idx
    incl = c_idx <= r_idx
    strict2 = jnp.concatenate([strict, strict], axis=1)
    incl2 = jnp.concatenate([incl, incl], axis=1)
    blk_r = lax.broadcasted_iota(jnp.int32, (LANES, LANES), 0) // HEAD_DIM
    blk_c = lax.broadcasted_iota(jnp.int32, (LANES, LANES), 1) // HEAD_DIM
    same_head = blk_r == blk_c

    n_pairs = w // LANES
    groups = [(n, pr) for n in range(nb) for pr in range(n_pairs)]
    ar, bk_chk, bk_hat, vp, g_mid, g_end = [], [], [], [], [], []
    for n in range(nb):
        rows = slice(n * c, (n + 1) * c)
        lcum = lcum_all[rows]
        lmid = lcum[c // 2 - 1:c // 2, :]
        lend = lcum[c - 1:c, :]
        lc = lcum - lmid
        g_inv = jnp.exp2(-lc)
        g_tail = jnp.exp2(lend - lcum)
        a_til = a_vec[rows] * jnp.exp2(lc - logw[rows])
        r_til = r[rows] * jnp.exp2(lc)
        b_chk = b_vec[rows] * g_inv
        k_chk = k[rows] * g_inv
        b_hat = b_vec[rows] * g_tail
        k_hat = k[rows] * g_tail
        gm, ge = jnp.exp2(lmid), jnp.exp2(lend)
        for pr in range(n_pairs):
            sl = slice(pr * LANES, (pr + 1) * LANES)
            ar.append(jnp.concatenate([a_til[:, sl], r_til[:, sl]], axis=0).astype(BF16))
            bk_chk.append(jnp.concatenate([_head_split(b_chk[:, sl].astype(BF16), lane_head),
                                           _head_split(k_chk[:, sl].astype(BF16), lane_head)], axis=0))
            bk_hat.append(jnp.concatenate([b_hat[:, sl], k_hat[:, sl]], axis=0).astype(BF16))
            vp.append(v[rows, sl])
            g_mid.append(gm[:, sl])
            g_end.append(ge[:, sl])
    v_split = [_head_split(t.astype(BF16), lane_head) for t in vp]
    state = [state_ref[n, pr] for n, pr in groups]

    g = [_mm_nt(ar[gi], bk_chk[gi]) for gi in range(len(groups))]
    x = [_mm_nt(ar[gi], state[gi] * g_mid[gi]) for gi in range(len(groups))]
    w_a = [jnp.where(strict2, gh[:c], 0.0) for gh in g]
    w_r = [jnp.where(incl2, gh[c:], 0.0).astype(BF16) for gh in g]
    ak_v = [_mm(w_a[gi][:, LANES:], v_split[gi]) for gi in range(len(groups))]
    t_inv = _tri_inverse_all([wa[:, :LANES] for wa in w_a], r_idx, c_idx, lane_head)
    u = [_mm(t_inv[gi], _head_split(x[gi][:c] + ak_v[gi], lane_head)) for gi in range(len(groups))]
    y_pair = [x[gi][c:] + _mm(w_r[gi], jnp.concatenate(
                  [_head_split(u[gi].astype(BF16), lane_head), v_split[gi]], axis=0))
              for gi in range(len(groups))]
    for gi, (n, pr) in enumerate(groups):
        uv = jnp.concatenate([u[gi], vp[gi]], axis=0)
        state_ref[n, pr] = (state[gi] * g_end[gi]
                            + jnp.where(same_head, _mm_tn(uv, bk_hat[gi]), 0.0))
    y = jnp.concatenate([jnp.concatenate(y_pair[n * n_pairs:(n + 1) * n_pairs], axis=1)
                         for n in range(nb)], axis=0)
    mean = _head_sums(y, hsum, 2) * (1.0 / HEAD_DIM)
    yc = y - mean
    var = _head_sums(yc * yc, hsum, 1) * (1.0 / HEAD_DIM)
    yn = yc * lax.rsqrt(var + GN_EPS) * ln_g + ln_b
    bonus = _head_sums(r * k * r_k, hsum, 1) * v
    out = ((yn + bonus) * gate).astype(o_ref.dtype)
    for n in range(nb):
        o_ref[n] = out[n * c:(n + 1) * c]


def _rwkv(rw3, mu, vec, wup, aup, gup, tril, hsum):
    b, s, cols = rw3.shape
    c = RWKV_CHUNK
    nb = RWKV_ROWS
    return pl.pallas_call(
        _rwkv_kernel,
        grid=(b // nb, s // c),
        in_specs=[pl.BlockSpec((nb, c, cols), lambda bb, i: (bb, i, 0)),
                  _const_spec(mu.shape), _const_spec(vec.shape), _const_spec(wup.shape),
                  _const_spec(aup.shape), _const_spec(gup.shape), _const_spec(tril.shape),
                  _const_spec(hsum.shape)],
        out_specs=pl.BlockSpec((nb, c, RWKV_W), lambda bb, i: (bb, i, 0)),
        out_shape=jax.ShapeDtypeStruct((b, s, RWKV_W), BF16),
        scratch_shapes=[pltpu.VMEM((nb, RWKV_W // LANES, LANES, LANES), F32),
                        pltpu.VMEM((nb, SUBLANES, cols), F32)],
        compiler_params=_compiler_params(2),
        name="rwkv7",
    )(rw3, mu, vec, wup, aup, gup, tril, hsum)


def _mlp_kernel(x_ref, att_ref, conv_ref, halo_ref, rw_ref, convw_ref, wo_att_ref, wo_conv_ref,
                wo_rw_ref, g_ref, wup_ref, wdn_ref, gfin_ref, o_ref, *, tiles_per_seq, final):
    tm = x_ref.shape[0]
    cw = CONV_W
    conv = conv_ref[...]
    zc = conv[:, cw:2 * cw] * conv[:, 2 * cw:]
    halo = halo_ref[...]
    seq_start = (pl.program_id(0) % tiles_per_seq) == 0
    zh = jnp.where(seq_start, 0.0, halo[:, cw:2 * cw] * halo[:, 2 * cw:])
    row = lax.broadcasted_iota(jnp.int32, (tm, 1), 0)
    z1 = jnp.where(row == 0, zh[SUBLANES - 1:SUBLANES], pltpu.roll(zc, 1, 0))
    z2 = jnp.where(row == 0, zh[SUBLANES - 2:SUBLANES - 1],
                   jnp.where(row == 1, zh[SUBLANES - 1:SUBLANES], pltpu.roll(zc, 2, 0)))
    cwt = convw_ref[...]
    o_conv = conv[:, :cw] * (cwt[0:1] * z2 + cwt[1:2] * z1 + cwt[2:3] * zc)

    x = x_ref[...]
    x = x + (jnp.dot(att_ref[...], wo_att_ref[...], preferred_element_type=F32)
             + jnp.dot(o_conv.astype(BF16), wo_conv_ref[...], preferred_element_type=F32)
             + jnp.dot(rw_ref[...], wo_rw_ref[...], preferred_element_type=F32))

    h = _rms(x, g_ref[...]).astype(BF16)
    d_ff = wup_ref.shape[1]
    acc = x
    for j in range(d_ff // FF_CHUNK):
        cols = slice(j * FF_CHUNK, (j + 1) * FF_CHUNK)
        up = jnp.maximum(jnp.dot(h, wup_ref[:, cols], preferred_element_type=F32), 0.0)
        acc = acc + jnp.dot((up * up).astype(BF16), wdn_ref[cols, :], preferred_element_type=F32)
    if final:
        acc = _rms(acc, gfin_ref[...])
    o_ref[...] = acc


def _mlp(x2, att2, conv2, rw2, convw, wo_att, wo_conv, wo_rw, g, wup, wdn, gfin, *, seq, final):
    t, d = x2.shape
    tm = ROW_TILE
    halo_blocks = tm // SUBLANES
    row = lambda n: pl.BlockSpec((tm, n), lambda i: (i, 0))
    halo = pl.BlockSpec((SUBLANES, conv2.shape[1]),
                        lambda i: (jnp.maximum(i * halo_blocks - 1, 0), 0))
    consts = [convw, wo_att, wo_conv, wo_rw, g, wup, wdn, gfin]
    return pl.pallas_call(
        functools.partial(_mlp_kernel, tiles_per_seq=seq // tm, final=final),
        grid=(t // tm,),
        in_specs=[row(d), row(att2.shape[1]), row(conv2.shape[1]), halo, row(rw2.shape[1])]
                 + [_const_spec(a.shape) for a in consts],
        out_specs=row(d),
        out_shape=jax.ShapeDtypeStruct((t, d), F32),
        compiler_params=_compiler_params(1),
        name="mix_mlp",
    )(x2, att2, conv2, conv2, rw2, *consts)


def kernel(x, norm_mix_g, w_in, lam_q1, lam_k1, lam_q2, lam_k2, subln_g, conv_w, shift_mu, rwkv_w0, rwkv_w_up, rwkv_a0, rwkv_a_up, rwkv_g_up, rwkv_k_k, rwkv_k_a, rwkv_r_k, lnx_g, lnx_b, w_out, norm_mlp_g, w_mlp_up, w_mlp_down, final_norm_g):
    bsz, seq, d = x.shape
    depth = w_in.shape[0]
    t = bsz * seq
    assert t % ROW_TILE == 0 and seq % ROW_TILE == 0 and seq % ATT_TILE == 0 and seq % RWKV_CHUNK == 0
    assert bsz % RWKV_ROWS == 0

    att_cols, conv_cols = 3 * ATT_W, 3 * CONV_W
    qscale = jnp.concatenate([jnp.full((1, ATT_W), DIFF_HALF ** -0.5 * math.log2(math.e), F32),
                              jnp.ones((1, ATT_W), F32)], axis=1)
    idx = jnp.arange(min(CUMSUM_BLOCK, RWKV_ROWS * RWKV_CHUNK))
    tril = jnp.logical_and(idx[None, :] <= idx[:, None],
                           idx[None, :] // RWKV_CHUNK == idx[:, None] // RWKV_CHUNK).astype(BF16)
    hidx = jnp.arange(LANES) // HEAD_DIM
    hsum = (hidx[:, None] == hidx[None, :]).astype(BF16)
    lora_pad = jnp.zeros((DECAY_LORA, RWKV_W), F32)

    x2 = x.reshape(t, d)
    for l in range(depth):
        lambda_init = 0.8 - 0.6 * math.exp(-0.3 * l)
        w_l = w_in[l].astype(BF16)
        qk2, vt, conv2, rw2 = _inproj(
            x2, norm_mix_g[l][None], qscale, w_l[:, :2 * ATT_W], w_l[:, 2 * ATT_W:att_cols].T,
            w_l[:, att_cols:att_cols + conv_cols], w_l[:, att_cols + conv_cols:])

        lam_params = jnp.stack([lam_q1[l], lam_k1[l], lam_q2[l], lam_k2[l]])
        o_att = _attention(qk2.reshape(bsz, seq, 2 * ATT_W), vt, lam_params, subln_g[l][:, None],
                           lambda_init)

        vec = jnp.stack([rwkv_w0[l], rwkv_a0[l], rwkv_k_k[l], rwkv_k_a[l], rwkv_r_k[l].reshape(-1),
                         lnx_g[l], lnx_b[l], jnp.zeros((RWKV_W,), F32)])
        wup = jnp.concatenate([rwkv_w_up[l], lora_pad], axis=0).astype(BF16)
        aup = jnp.concatenate([lora_pad, rwkv_a_up[l]], axis=0).astype(BF16)
        o_rw = _rwkv(rw2.reshape(bsz, seq, RWKV_COLS), shift_mu[l][None], vec, wup, aup,
                     rwkv_g_up[l].astype(BF16), tril, hsum)

        wo = w_out[l].astype(BF16)
        x2 = _mlp(x2, o_att.reshape(t, ATT_W), conv2, o_rw.reshape(t, RWKV_W), conv_w[l],
                  wo[:ATT_W], wo[ATT_W:ATT_W + CONV_W], wo[ATT_W + CONV_W:], norm_mlp_g[l][None],
                  w_mlp_up[l].astype(BF16), w_mlp_down[l].astype(BF16), final_norm_g[None],
                  seq=seq, final=(l == depth - 1))
    return x2.reshape(bsz, seq, d)
```

```python
import functools
import math

import jax
import jax.numpy as jnp
from jax import lax
from jax.experimental import pallas as pl
from jax.experimental.pallas import tpu as pltpu

F32 = jnp.float32
BF16 = jnp.bfloat16

HEAD_DIM = 64
DIFF_HALF = HEAD_DIM // 2
ATT_HEADS = 6
CONV_GROUPS = 4
RWKV_HEADS = 6
ATT_W = ATT_HEADS * HEAD_DIM
CONV_W = CONV_GROUPS * HEAD_DIM
RWKV_W = RWKV_HEADS * HEAD_DIM
CONV_K = 3
DECAY_LORA = 64
ICLR_LORA = 64
GATE_LORA = 128
RWKV_COLS = 3 * RWKV_W + DECAY_LORA + ICLR_LORA + GATE_LORA
NORM_EPS = 1e-6
GN_EPS = 64e-5

LANES = 128
SUBLANES = 8
HEADS_PER_VREG = LANES // HEAD_DIM
VMEM_LIMIT_BYTES = 56 * 1024 * 1024

ROW_TILE = 512
ATT_TILE = 1024
KEY_PARTS = 4
DIAG_STRIPS = 4
RWKV_CHUNK = 64
RWKV_ROWS = 8
CUMSUM_BLOCK = 256
FF_CHUNK = 1024
NEG_BIG = -1e30
ONES_ROWS = 16


def _compiler_params(n_axes):
    return pltpu.CompilerParams(
        dimension_semantics=("arbitrary",) * n_axes,
        vmem_limit_bytes=VMEM_LIMIT_BYTES,
    )


def _const_spec(shape):
    nd = len(shape)
    return pl.BlockSpec(shape, lambda *_: (0,) * nd)


def _layer_spec(shape, layer):
    nd = len(shape)
    return pl.BlockSpec((1,) + tuple(shape[1:]), lambda *_: (layer,) + (0,) * (nd - 1))


def _mm(a, b):
    return jnp.dot(a.astype(BF16), b.astype(BF16), preferred_element_type=F32)


def _mm_nt(a, b):
    return lax.dot_general(a.astype(BF16), b.astype(BF16), (((1,), (1,)), ((), ())),
                           preferred_element_type=F32)


def _mm_tn(a, b):
    return lax.dot_general(a.astype(BF16), b.astype(BF16), (((0,), (0,)), ((), ())),
                           preferred_element_type=F32)


def _bf16_pieces(x, terms):
    pieces = []
    rem = x
    for t in range(terms):
        piece = rem.astype(BF16)
        pieces.append(piece)
        if t + 1 < terms:
            rem = rem - piece.astype(F32)
    return pieces


def _head_sums(x, ones_bf16, terms):
    out = []
    for s in range(x.shape[1] // LANES):
        xs = x[:, s * LANES:(s + 1) * LANES]
        out.append(functools.reduce(lambda a, b: a + b, [
            jnp.dot(piece, ones_bf16, preferred_element_type=F32) for piece in _bf16_pieces(xs, terms)]))
    return jnp.concatenate(out, axis=1)


def _cumsum_rows(tril_bf16, x):
    blk = tril_bf16.shape[0]
    out = []
    for s in range(x.shape[0] // blk):
        xs = x[s * blk:(s + 1) * blk]
        out.append(functools.reduce(lambda a, b: a + b, [
            jnp.dot(tril_bf16, piece, preferred_element_type=F32) for piece in _bf16_pieces(xs, 3)]))
    return jnp.concatenate(out, axis=0)


def _sigmoid(x):
    return 0.5 * jnp.tanh(0.5 * x) + 0.5


def _rms(x, g):
    ms = jnp.mean(x * x, axis=-1, keepdims=True)
    return x * lax.rsqrt(ms + NORM_EPS) * g


def _inproj_kernel(x_ref, g_ref, qscale_ref, w_ref, w_vt_ref, qk_ref, vt_ref, conv_ref, rw_ref):
    h = _rms(x_ref[...], g_ref[0]).astype(BF16)
    qk_end, v_end, conv_end = 2 * ATT_W, 3 * ATT_W, 3 * ATT_W + 3 * CONV_W
    qk = jnp.dot(h, w_ref[0, :, :qk_end], preferred_element_type=F32)
    qk_ref[...] = (qk * qscale_ref[...]).astype(BF16)
    vt_ref[...] = _mm_nt(w_vt_ref[0], h).astype(BF16)
    conv_ref[...] = jnp.dot(h, w_ref[0, :, v_end:conv_end], preferred_element_type=F32)
    rw_ref[...] = jnp.dot(h, w_ref[0, :, conv_end:], preferred_element_type=F32)


def _inproj(x2, g, qscale, w_in, w_vt, layer):
    t, d = x2.shape
    tm = ROW_TILE
    row = lambda n: pl.BlockSpec((tm, n), lambda i: (i, 0))
    return pl.pallas_call(
        _inproj_kernel,
        grid=(t // tm,),
        in_specs=[row(d), _layer_spec(g.shape, layer), _const_spec(qscale.shape),
                  _layer_spec(w_in.shape, layer), _layer_spec(w_vt.shape, layer)],
        out_specs=[row(2 * ATT_W), pl.BlockSpec((ATT_W, tm), lambda i: (0, i)),
                   row(3 * CONV_W), row(RWKV_COLS)],
        out_shape=[jax.ShapeDtypeStruct((t, 2 * ATT_W), BF16),
                   jax.ShapeDtypeStruct((ATT_W, t), BF16),
                   jax.ShapeDtypeStruct((t, 3 * CONV_W), F32),
                   jax.ShapeDtypeStruct((t, RWKV_COLS), F32)],
        compiler_params=_compiler_params(1),
        name="inproj",
    )(x2, g, qscale, w_in, w_vt)


def _attn_kernel(lam_ref, sg_ref, q_ref, k_ref, vt_ref, o_ref, *, lambda_init):
    tq = q_ref.shape[1]
    i = pl.program_id(2)
    q = q_ref[0]
    lane = lax.broadcasted_iota(jnp.int32, (tq, LANES), 1)
    n_maps = LANES // DIFF_HALF
    maps = range(n_maps)
    q_sel = [jnp.where(lane // DIFF_HALF == c, q, jnp.zeros_like(q)) for c in maps]

    def scores(k_start, tk, q_lo=0, k_off=None):
        ks = k_ref[0, pl.ds(k_start, tk), :]
        out = [_mm_nt(ks, q_sel[c][q_lo:]) for c in maps]
        if k_off is not None:
            key_pos = lax.broadcasted_iota(jnp.int32, (tk, tq - q_lo), 0) + k_off
            qry_pos = lax.broadcasted_iota(jnp.int32, (tk, tq - q_lo), 1) + q_lo
            out = [jnp.where(key_pos <= qry_pos, s, -jnp.inf) for s in out]
        return out

    def softmax(s, m):
        m_new = [jnp.maximum(m[c], jnp.max(s[c], axis=0, keepdims=True)) for c in maps]
        alpha = [jnp.exp2(m[c] - m_new[c]) for c in maps]
        p = [jnp.exp2(s[c] - m_new[c]).astype(BF16) for c in maps]
        return m_new, alpha, p

    def values(k_start, tk, p):
        vt = vt_ref[:, pl.ds(k_start, tk)]
        ones = jnp.ones((ONES_ROWS, tk), BF16)
        vt_aug = [jnp.concatenate([vt[hh * HEAD_DIM:(hh + 1) * HEAD_DIM], ones], axis=0)
                  for hh in range(HEADS_PER_VREG)]
        return [jnp.dot(vt_aug[c // 2], p[c], preferred_element_type=F32) for c in maps]

    def fold(k_start, tk, carry, q_lo=0, k_off=None, parts=1):
        sub = tk // parts
        s_parts = [scores(k_start + n * sub, sub, q_lo, None if k_off is None else k_off + n * sub)
                   for n in range(parts)]
        m, l, acc = carry
        m_new = [m[c][:, q_lo:] for c in maps]
        l_new = [l[c][:, q_lo:] for c in maps]
        acc_new = [acc[c][:, q_lo:] for c in maps]
        for n in range(parts):
            m_new, alpha, p = softmax(s_parts[n], m_new)
            pv = values(k_start + n * sub, sub, p)
            l_new = [alpha[c] * l_new[c] + pv[c][HEAD_DIM:HEAD_DIM + 1] for c in maps]
            acc_new = [alpha[c] * acc_new[c] + pv[c][:HEAD_DIM] for c in maps]
        if q_lo:
            keep = lambda old, new: [jnp.concatenate([old[c][:, :q_lo], new[c]], axis=1) for c in maps]
            m_new, l_new, acc_new = keep(m, m_new), keep(l, l_new), keep(acc, acc_new)
        return m_new, l_new, acc_new

    carry = ([jnp.full((1, tq), NEG_BIG, F32) for _ in maps], [jnp.zeros((1, tq), F32) for _ in maps],
             [jnp.zeros((HEAD_DIM, tq), F32) for _ in maps])
    carry = lax.fori_loop(
        0, i, lambda j, cr: fold(pl.multiple_of(j * tq, tq), tq, cr, parts=KEY_PARTS), carry)
    strip = tq // DIAG_STRIPS
    diag = pl.multiple_of(i * tq, tq)
    for n in range(DIAG_STRIPS):
        carry = fold(diag + n * strip, strip, carry, q_lo=n * strip, k_off=n * strip)
    _, l, acc = carry

    lp = lam_ref[0]
    lam = (jnp.exp(jnp.sum(lp[0:1] * lp[1:2], axis=-1, keepdims=True))
           - jnp.exp(jnp.sum(lp[2:3] * lp[3:4], axis=-1, keepdims=True)) + lambda_init)
    heads = []
    for hh in range(HEADS_PER_VREG):
        o = acc[2 * hh] * (1.0 / l[2 * hh]) - lam * (acc[2 * hh + 1] * (1.0 / l[2 * hh + 1]))
        ms = jnp.mean(o * o, axis=0, keepdims=True)
        heads.append(o * lax.rsqrt(ms + NORM_EPS) * sg_ref[0] * (1.0 - lambda_init))
    o_ref[0] = jnp.concatenate(heads, axis=0).T.astype(o_ref.dtype)


def _attention(qk3, vt, lam_params, subln_col, layer, lambda_init):
    b, s, _ = qk3.shape
    tq = ATT_TILE
    n_pairs = ATT_W // LANES
    return pl.pallas_call(
        functools.partial(_attn_kernel, lambda_init=lambda_init),
        grid=(b, n_pairs, s // tq),
        in_specs=[_layer_spec(lam_params.shape, layer), _layer_spec(subln_col.shape, layer),
                  pl.BlockSpec((1, tq, LANES), lambda bb, hp, i: (bb, i, hp)),
                  pl.BlockSpec((1, s, LANES), lambda bb, hp, i: (bb, 0, n_pairs + hp)),
                  pl.BlockSpec((LANES, s), lambda bb, hp, i: (hp, bb))],
        out_specs=pl.BlockSpec((1, tq, LANES), lambda bb, hp, i: (bb, i, hp)),
        out_shape=jax.ShapeDtypeStruct((b, s, ATT_W), BF16),
        compiler_params=_compiler_params(3),
        name="diff_attn",
    )(lam_params, subln_col, qk3, qk3, vt)


def _head_split(x, lane_head):
    return jnp.concatenate([jnp.where(lane_head == hh, x, jnp.zeros_like(x))
                            for hh in range(HEADS_PER_VREG)], axis=0)


def _tri_inverse_all(mats, r_idx, c_idx, lane_head):
    c = mats[0].shape[0]
    base = 16
    same = lambda blk: (r_idx // blk) == (c_idx // blk)
    eye = (r_idx == c_idx).astype(F32)
    mul = lambda ls, rs: [_mm(l, _head_split(r, lane_head)) for l, r in zip(ls, rs)]
    in_base = same(base)
    diag = [jnp.where(in_base, a, 0.0) for a in mats]
    powers = mul(diag, diag)
    invs = [eye + d for d in diag]
    span = 2
    while 2 * span < base:
        both = mul([jnp.concatenate([pw, inv], axis=0) for pw, inv in zip(powers, invs)], powers)
        powers = [t[:c] for t in both]
        invs = [inv + t[c:] for inv, t in zip(invs, both)]
        span *= 2
    invs = [inv + t for inv, t in zip(invs, mul(invs, powers))]
    blk = base
    while blk < c:
        off_diag = jnp.logical_and(same(2 * blk), jnp.logical_not(same(blk)))
        tmp = mul([jnp.where(off_diag, a, 0.0) for a in mats], invs)
        invs = [inv + t for inv, t in zip(invs, mul(invs, tmp))]
        blk *= 2
    return invs


def _rwkv_kernel(p_ref, mu_ref, vec_ref, wup_ref, aup_ref, gup_ref, tril_ref, hsum_ref,
                 o_ref, state_ref, prev_ref):
    nb, c, _ = p_ref.shape
    w = RWKV_W
    assert HEADS_PER_VREG * c == LANES

    @pl.when(pl.program_id(1) == 0)
    def _():
        state_ref[...] = jnp.zeros_like(state_ref)
        prev_ref[...] = jnp.zeros_like(prev_ref)

    p = jnp.concatenate([p_ref[n] for n in range(nb)], axis=0)
    row1 = lax.broadcasted_iota(jnp.int32, (nb * c, 1), 0)
    prev = pltpu.roll(p, 1, 0)
    for n in range(nb):
        prev = jnp.where(row1 == n * c, prev_ref[n, SUBLANES - 1:SUBLANES, :], prev)
        prev_ref[n] = p[(n + 1) * c - SUBLANES:(n + 1) * c, :]
    z = p + mu_ref[0] * (prev - p)

    vec = vec_ref[0]
    w0, a0, k_k, k_a, r_k, ln_g, ln_b = (vec[n:n + 1] for n in range(7))
    r = z[:, 0:w]
    k = z[:, w:2 * w]
    v = z[:, 2 * w:3 * w]
    lora_in = z[:, 3 * w:3 * w + DECAY_LORA + ICLR_LORA]
    g_in = z[:, 3 * w + DECAY_LORA + ICLR_LORA:]

    wdec = w0 + _mm(jnp.tanh(lora_in), wup_ref[0])
    logw = -(math.exp(-0.5) * math.log2(math.e)) * _sigmoid(wdec)
    iclr = _sigmoid(a0 + _mm(lora_in, aup_ref[0]))
    gate = _mm(_sigmoid(g_in), gup_ref[0])

    hsum = hsum_ref[...]
    kk = k * k_k
    kk = kk * lax.rsqrt(jnp.maximum(_head_sums(kk * kk, hsum, 1), 1e-24))
    k = k * (1.0 + (iclr - 1.0) * k_a)
    a_vec = -kk
    b_vec = kk * iclr
    lcum_all = _cumsum_rows(tril_ref[...], logw)

    r_idx = lax.broadcasted_iota(jnp.int32, (c, LANES), 0)
    lane = lax.broadcasted_iota(jnp.int32, (c, LANES), 1)
    c_idx = lane % c
    lane_head = lane // HEAD_DIM
    strict = c_idx < r_idx
    incl = c_idx <= r_idx
    strict2 = jnp.concatenate([strict, strict], axis=1)
    incl2 = jnp.concatenate([incl, incl], axis=1)
    blk_r = lax.broadcasted_iota(jnp.int32, (LANES, LANES), 0) // HEAD_DIM
    blk_c = lax.broadcasted_iota(jnp.int32, (LANES, LANES), 1) // HEAD_DIM
    same_head = blk_r == blk_c

    n_pairs = w // LANES
    groups = [(n, pr) for n in range(nb) for pr in range(n_pairs)]
    ar, bk_chk, bk_hat, vp, g_mid, g_end = [], [], [], [], [], []
    for n in range(nb):
        rows = slice(n * c, (n + 1) * c)
        lcum = lcum_all[rows]
        lmid = lcum[c // 2 - 1:c // 2, :]
        lend = lcum[c - 1:c, :]
        lc = lcum - lmid
        g_inv = jnp.exp2(-lc)
        g_tail = jnp.exp2(lend - lcum)
        a_til = a_vec[rows] * jnp.exp2(lc - logw[rows])
        r_til = r[rows] * jnp.exp2(lc)
        b_chk = b_vec[rows] * g_inv
        k_chk = k[rows] * g_inv
        b_hat = b_vec[rows] * g_tail
        k_hat = k[rows] * g_tail
        gm, ge = jnp.exp2(lmid), jnp.exp2(lend)
        for pr in range(n_pairs):
            sl = slice(pr * LANES, (pr + 1) * LANES)
            ar.append(jnp.concatenate([a_til[:, sl], r_til[:, sl]], axis=0).astype(BF16))
            bk_chk.append(jnp.concatenate([_head_split(b_chk[:, sl].astype(BF16), lane_head),
                                           _head_split(k_chk[:, sl].astype(BF16), lane_head)], axis=0))
            bk_hat.append(jnp.concatenate([b_hat[:, sl], k_hat[:, sl]], axis=0).astype(BF16))
            vp.append(v[rows, sl])
            g_mid.append(gm[:, sl])
            g_end.append(ge[:, sl])
    v_split = [_head_split(t.astype(BF16), lane_head) for t in vp]
    state = [state_ref[n, pr] for n, pr in groups]

    g = [_mm_nt(ar[gi], bk_chk[gi]) for gi in range(len(groups))]
    x = [_mm_nt(ar[gi], state[gi] * g_mid[gi]) for gi in range(len(groups))]
    w_a = [jnp.where(strict2, gh[:c], 0.0) for gh in g]
    w_r = [jnp.where(incl2, gh[c:], 0.0).astype(BF16) for gh in g]
    ak_v = [_mm(w_a[gi][:, LANES:], v_split[gi]) for gi in range(len(groups))]
    t_inv = _tri_inverse_all([wa[:, :LANES] for wa in w_a], r_idx, c_idx, lane_head)
    u = [_mm(t_inv[gi], _head_split(x[gi][:c] + ak_v[gi], lane_head)) for gi in range(len(groups))]
    y_pair = [x[gi][c:] + _mm(w_r[gi], jnp.concatenate(
                  [_head_split(u[gi].astype(BF16), lane_head), v_split[gi]], axis=0))
              for gi in range(len(groups))]
    for gi, (n, pr) in enumerate(groups):
        uv = jnp.concatenate([u[gi], vp[gi]], axis=0)
        state_ref[n, pr] = (state[gi] * g_end[gi]
                            + jnp.where(same_head, _mm_tn(uv, bk_hat[gi]), 0.0))
    y = jnp.concatenate([jnp.concatenate(y_pair[n * n_pairs:(n + 1) * n_pairs], axis=1)
                         for n in range(nb)], axis=0)
    mean = _head_sums(y, hsum, 2) * (1.0 / HEAD_DIM)
    yc = y - mean
    var = _head_sums(yc * yc, hsum, 1) * (1.0 / HEAD_DIM)
    yn = yc * lax.rsqrt(var + GN_EPS) * ln_g + ln_b
    bonus = _head_sums(r * k * r_k, hsum, 1) * v
    out = ((yn + bonus) * gate).astype(o_ref.dtype)
    for n in range(nb):
        o_ref[n] = out[n * c:(n + 1) * c]


def _rwkv(rw3, mu, vec, wup, aup, gup, tril, hsum, layer):
    b, s, cols = rw3.shape
    c = RWKV_CHUNK
    nb = RWKV_ROWS
    return pl.pallas_call(
        _rwkv_kernel,
        grid=(b // nb, s // c),
        in_specs=[pl.BlockSpec((nb, c, cols), lambda bb, i: (bb, i, 0)),
                  _layer_spec(mu.shape, layer), _layer_spec(vec.shape, layer),
                  _layer_spec(wup.shape, layer), _layer_spec(aup.shape, layer),
                  _layer_spec(gup.shape, layer), _const_spec(tril.shape), _const_spec(hsum.shape)],
        out_specs=pl.BlockSpec((nb, c, RWKV_W), lambda bb, i: (bb, i, 0)),
        out_shape=jax.ShapeDtypeStruct((b, s, RWKV_W), BF16),
        scratch_shapes=[pltpu.VMEM((nb, RWKV_W // LANES, LANES, LANES), F32),
                        pltpu.VMEM((nb, SUBLANES, cols), F32)],
        compiler_params=_compiler_params(2),
        name="rwkv7",
    )(rw3, mu, vec, wup, aup, gup, tril, hsum)


def _mlp_kernel(x_ref, att_ref, conv_ref, halo_ref, rw_ref, convw_ref, wo_ref, g_ref, wup_ref, wdn_ref,
                gfin_ref, o_ref, *, tiles_per_seq, final):
    tm = x_ref.shape[0]
    cw = CONV_W
    conv = conv_ref[...]
    zc = conv[:, cw:2 * cw] * conv[:, 2 * cw:]
    halo = halo_ref[...]
    seq_start = (pl.program_id(0) % tiles_per_seq) == 0
    zh = jnp.where(seq_start, 0.0, halo[:, cw:2 * cw] * halo[:, 2 * cw:])
    row = lax.broadcasted_iota(jnp.int32, (tm, 1), 0)
    z1 = jnp.where(row == 0, zh[SUBLANES - 1:SUBLANES], pltpu.roll(zc, 1, 0))
    z2 = jnp.where(row == 0, zh[SUBLANES - 2:SUBLANES - 1],
                   jnp.where(row == 1, zh[SUBLANES - 1:SUBLANES], pltpu.roll(zc, 2, 0)))
    cwt = convw_ref[0]
    o_conv = conv[:, :cw] * (cwt[0:1] * z2 + cwt[1:2] * z1 + cwt[2:3] * zc)

    x = x_ref[...]
    x = x + (jnp.dot(att_ref[...], wo_ref[0, :ATT_W], preferred_element_type=F32)
             + jnp.dot(o_conv.astype(BF16), wo_ref[0, ATT_W:ATT_W + cw], preferred_element_type=F32)
             + jnp.dot(rw_ref[...], wo_ref[0, ATT_W + cw:], preferred_element_type=F32))

    h = _rms(x, g_ref[0]).astype(BF16)
    d_ff = wup_ref.shape[2]
    acc = x
    for j in range(d_ff // FF_CHUNK):
        cols = slice(j * FF_CHUNK, (j + 1) * FF_CHUNK)
        up = jnp.maximum(jnp.dot(h, wup_ref[0, :, cols], preferred_element_type=F32), 0.0)
        acc = acc + jnp.dot((up * up).astype(BF16), wdn_ref[0, cols, :], preferred_element_type=F32)
    if final:
        acc = _rms(acc, gfin_ref[...])
    o_ref[...] = acc


def _mlp(x2, att2, conv2, rw2, convw, wo, g, wup, wdn, gfin, *, layer, seq, final):
    t, d = x2.shape
    tm = ROW_TILE
    halo_blocks = tm // SUBLANES
    row = lambda n: pl.BlockSpec((tm, n), lambda i: (i, 0))
    halo = pl.BlockSpec((SUBLANES, conv2.shape[1]),
                        lambda i: (jnp.maximum(i * halo_blocks - 1, 0), 0))
    params = [convw, wo, g, wup, wdn]
    return pl.pallas_call(
        functools.partial(_mlp_kernel, tiles_per_seq=seq // tm, final=final),
        grid=(t // tm,),
        in_specs=[row(d), row(att2.shape[1]), row(conv2.shape[1]), halo, row(rw2.shape[1])]
                 + [_layer_spec(a.shape, layer if a.shape[0] > 1 else 0) for a in params]
                 + [_const_spec(gfin.shape)],
        out_specs=row(d),
        out_shape=jax.ShapeDtypeStruct((t, d), F32),
        compiler_params=_compiler_params(1),
        name="mix_mlp",
    )(x2, att2, conv2, conv2, rw2, *params, gfin)


def kernel(x, norm_mix_g, w_in, lam_q1, lam_k1, lam_q2, lam_k2, subln_g, conv_w, shift_mu, rwkv_w0, rwkv_w_up, rwkv_a0, rwkv_a_up, rwkv_g_up, rwkv_k_k, rwkv_k_a, rwkv_r_k, lnx_g, lnx_b, w_out, norm_mlp_g, w_mlp_up, w_mlp_down, final_norm_g):
    bsz, seq, d = x.shape
    depth = w_in.shape[0]
    t = bsz * seq
    assert t % ROW_TILE == 0 and seq % ROW_TILE == 0 and seq % ATT_TILE == 0 and seq % RWKV_CHUNK == 0
    assert bsz % RWKV_ROWS == 0

    att_cols = 3 * ATT_W
    qscale = jnp.concatenate([jnp.full((1, ATT_W), DIFF_HALF ** -0.5 * math.log2(math.e), F32),
                              jnp.ones((1, ATT_W), F32)], axis=1)
    idx = jnp.arange(min(CUMSUM_BLOCK, RWKV_ROWS * RWKV_CHUNK))
    tril = jnp.logical_and(idx[None, :] <= idx[:, None],
                           idx[None, :] // RWKV_CHUNK == idx[:, None] // RWKV_CHUNK).astype(BF16)
    hidx = jnp.arange(LANES) // HEAD_DIM
    hsum = (hidx[:, None] == hidx[None, :]).astype(BF16)
    w_in_bf = w_in.astype(BF16)
    w_vt_bf = jnp.swapaxes(w_in_bf[:, :, 2 * ATT_W:att_cols], 1, 2)
    w_out_bf = w_out.astype(BF16)
    lam_params = jnp.stack([lam_q1, lam_k1, lam_q2, lam_k2], axis=1)
    vec = jnp.stack([rwkv_w0, rwkv_a0, rwkv_k_k, rwkv_k_a, rwkv_r_k.reshape(depth, RWKV_W), lnx_g, lnx_b,
                     jnp.zeros((depth, RWKV_W), F32)], axis=1)
    lora_pad = jnp.zeros((depth, DECAY_LORA, RWKV_W), F32)
    wup = jnp.concatenate([rwkv_w_up, lora_pad], axis=1).astype(BF16)
    aup = jnp.concatenate([lora_pad, rwkv_a_up], axis=1).astype(BF16)
    gup = rwkv_g_up.astype(BF16)

    x2 = x.reshape(t, d)
    for l in range(depth):
        lambda_init = 0.8 - 0.6 * math.exp(-0.3 * l)
        qk2, vt, conv2, rw2 = _inproj(x2, norm_mix_g[:, None], qscale, w_in_bf, w_vt_bf, l)
        o_att = _attention(qk2.reshape(bsz, seq, 2 * ATT_W), vt, lam_params, subln_g[:, :, None], l,
                           lambda_init)
        o_rw = _rwkv(rw2.reshape(bsz, seq, RWKV_COLS), shift_mu[:, None], vec, wup, aup, gup, tril, hsum, l)
        x2 = _mlp(x2, o_att.reshape(t, ATT_W), conv2, o_rw.reshape(t, RWKV_W), conv_w, w_out_bf,
                  norm_mlp_g[:, None], w_mlp_up[l:l + 1].astype(BF16), w_mlp_down[l:l + 1].astype(BF16),
                  final_norm_g[None],
                  layer=l, seq=seq, final=(l == depth - 1))
    return x2.reshape(bsz, seq, d)
```

```python
import functools
import math

import jax
import jax.numpy as jnp
from jax import lax
from jax.experimental import pallas as pl
from jax.experimental.pallas import tpu as pltpu

F32 = jnp.float32
BF16 = jnp.bfloat16

HEAD_DIM = 64
DIFF_HALF = HEAD_DIM // 2
ATT_HEADS = 6
CONV_GROUPS = 4
RWKV_HEADS = 6
ATT_W = ATT_HEADS * HEAD_DIM
CONV_W = CONV_GROUPS * HEAD_DIM
RWKV_W = RWKV_HEADS * HEAD_DIM
CONV_K = 3
DECAY_LORA = 64
ICLR_LORA = 64
GATE_LORA = 128
RWKV_COLS = 3 * RWKV_W + DECAY_LORA + ICLR_LORA + GATE_LORA
NORM_EPS = 1e-6
GN_EPS = 64e-5

LANES = 128
SUBLANES = 8
HEADS_PER_VREG = LANES // HEAD_DIM
VMEM_LIMIT_BYTES = 56 * 1024 * 1024

ROW_TILE = 512
ATT_TILE = 1024
KEY_PARTS = 4
DIAG_STRIPS = 4
RWKV_CHUNK = 64
RWKV_ROWS = 8
CUMSUM_BLOCK = 256
FF_CHUNK = 1024
NEG_BIG = -1e30
ONES_ROWS = 16


def _compiler_params(n_axes):
    return pltpu.CompilerParams(
        dimension_semantics=("arbitrary",) * n_axes,
        vmem_limit_bytes=VMEM_LIMIT_BYTES,
    )


def _const_spec(shape):
    nd = len(shape)
    return pl.BlockSpec(shape, lambda *_: (0,) * nd)


def _layer_spec(shape, layer):
    nd = len(shape)
    index = layer if shape[0] > 1 else 0
    return pl.BlockSpec((1,) + tuple(shape[1:]), lambda *_: (index,) + (0,) * (nd - 1))


def _mm(a, b):
    return jnp.dot(a.astype(BF16), b.astype(BF16), preferred_element_type=F32)


def _mm_nt(a, b):
    return lax.dot_general(a.astype(BF16), b.astype(BF16), (((1,), (1,)), ((), ())),
                           preferred_element_type=F32)


def _mm_tn(a, b):
    return lax.dot_general(a.astype(BF16), b.astype(BF16), (((0,), (0,)), ((), ())),
                           preferred_element_type=F32)


def _bf16_pieces(x, terms):
    pieces = []
    rem = x
    for t in range(terms):
        piece = rem.astype(BF16)
        pieces.append(piece)
        if t + 1 < terms:
            rem = rem - piece.astype(F32)
    return pieces


def _head_sums(x, ones_bf16, terms):
    out = []
    for s in range(x.shape[1] // LANES):
        xs = x[:, s * LANES:(s + 1) * LANES]
        out.append(functools.reduce(lambda a, b: a + b, [
            jnp.dot(piece, ones_bf16, preferred_element_type=F32) for piece in _bf16_pieces(xs, terms)]))
    return jnp.concatenate(out, axis=1)


def _cumsum_rows(tril_bf16, x):
    blk = tril_bf16.shape[0]
    out = []
    for s in range(x.shape[0] // blk):
        xs = x[s * blk:(s + 1) * blk]
        out.append(functools.reduce(lambda a, b: a + b, [
            jnp.dot(tril_bf16, piece, preferred_element_type=F32) for piece in _bf16_pieces(xs, 3)]))
    return jnp.concatenate(out, axis=0)


def _sigmoid(x):
    return 0.5 * jnp.tanh(0.5 * x) + 0.5


def _rms(x, g):
    ms = jnp.mean(x * x, axis=-1, keepdims=True)
    return x * lax.rsqrt(ms + NORM_EPS) * g


def _inproj_kernel(x_ref, g_ref, qscale_ref, w_ref, w_vt_ref, qk_ref, vt_ref, conv_ref, rw_ref):
    h = _rms(x_ref[...], g_ref[0]).astype(BF16)
    qk_end, v_end, conv_end = 2 * ATT_W, 3 * ATT_W, 3 * ATT_W + 3 * CONV_W
    qk = jnp.dot(h, w_ref[0, :, :qk_end], preferred_element_type=F32)
    qk_ref[...] = (qk * qscale_ref[...]).astype(BF16)
    vt_ref[...] = _mm_nt(w_vt_ref[0], h).astype(BF16)
    conv_ref[...] = jnp.dot(h, w_ref[0, :, v_end:conv_end], preferred_element_type=F32)
    rw_ref[...] = jnp.dot(h, w_ref[0, :, conv_end:], preferred_element_type=F32)


def _inproj(x2, g, qscale, w_in, w_vt, layer):
    t, d = x2.shape
    tm = ROW_TILE
    row = lambda n: pl.BlockSpec((tm, n), lambda i: (i, 0))
    return pl.pallas_call(
        _inproj_kernel,
        grid=(t // tm,),
        in_specs=[row(d), _layer_spec(g.shape, layer), _const_spec(qscale.shape),
                  _layer_spec(w_in.shape, layer), _layer_spec(w_vt.shape, layer)],
        out_specs=[row(2 * ATT_W), pl.BlockSpec((ATT_W, tm), lambda i: (0, i)),
                   row(3 * CONV_W), row(RWKV_COLS)],
        out_shape=[jax.ShapeDtypeStruct((t, 2 * ATT_W), BF16),
                   jax.ShapeDtypeStruct((ATT_W, t), BF16),
                   jax.ShapeDtypeStruct((t, 3 * CONV_W), F32),
                   jax.ShapeDtypeStruct((t, RWKV_COLS), F32)],
        compiler_params=_compiler_params(1),
        name="inproj",
    )(x2, g, qscale, w_in, w_vt)


def _attn_kernel(lam_ref, sg_ref, q_ref, k_ref, vt_ref, o_ref, *, lambda_init):
    tq = q_ref.shape[1]
    i = pl.program_id(2)
    q = q_ref[0]
    lane = lax.broadcasted_iota(jnp.int32, (tq, LANES), 1)
    n_maps = LANES // DIFF_HALF
    maps = range(n_maps)
    q_sel = [jnp.where(lane // DIFF_HALF == c, q, jnp.zeros_like(q)) for c in maps]

    def scores(k_start, tk, q_lo=0, k_off=None):
        ks = k_ref[0, pl.ds(k_start, tk), :]
        out = [_mm_nt(ks, q_sel[c][q_lo:]) for c in maps]
        if k_off is not None:
            key_pos = lax.broadcasted_iota(jnp.int32, (tk, tq - q_lo), 0) + k_off
            qry_pos = lax.broadcasted_iota(jnp.int32, (tk, tq - q_lo), 1) + q_lo
            out = [jnp.where(key_pos <= qry_pos, s, -jnp.inf) for s in out]
        return out

    def softmax(s, m):
        m_new = [jnp.maximum(m[c], jnp.max(s[c], axis=0, keepdims=True)) for c in maps]
        alpha = [jnp.exp2(m[c] - m_new[c]) for c in maps]
        p = [jnp.exp2(s[c] - m_new[c]).astype(BF16) for c in maps]
        return m_new, alpha, p

    def values(k_start, tk, p):
        vt = vt_ref[:, pl.ds(k_start, tk)]
        ones = jnp.ones((ONES_ROWS, tk), BF16)
        vt_aug = [jnp.concatenate([vt[hh * HEAD_DIM:(hh + 1) * HEAD_DIM], ones], axis=0)
                  for hh in range(HEADS_PER_VREG)]
        return [jnp.dot(vt_aug[c // 2], p[c], preferred_element_type=F32) for c in maps]

    def fold(k_start, tk, carry, q_lo=0, k_off=None, parts=1):
        sub = tk // parts
        s_parts = [scores(k_start + n * sub, sub, q_lo, None if k_off is None else k_off + n * sub)
                   for n in range(parts)]
        m, l, acc = carry
        m_new = [m[c][:, q_lo:] for c in maps]
        l_new = [l[c][:, q_lo:] for c in maps]
        acc_new = [acc[c][:, q_lo:] for c in maps]
        for n in range(parts):
            m_new, alpha, p = softmax(s_parts[n], m_new)
            pv = values(k_start + n * sub, sub, p)
            l_new = [alpha[c] * l_new[c] + pv[c][HEAD_DIM:HEAD_DIM + 1] for c in maps]
            acc_new = [alpha[c] * acc_new[c] + pv[c][:HEAD_DIM] for c in maps]
        if q_lo:
            keep = lambda old, new: [jnp.concatenate([old[c][:, :q_lo], new[c]], axis=1) for c in maps]
            m_new, l_new, acc_new = keep(m, m_new), keep(l, l_new), keep(acc, acc_new)
        return m_new, l_new, acc_new

    carry = ([jnp.full((1, tq), NEG_BIG, F32) for _ in maps], [jnp.zeros((1, tq), F32) for _ in maps],
             [jnp.zeros((HEAD_DIM, tq), F32) for _ in maps])
    carry = lax.fori_loop(
        0, i, lambda j, cr: fold(pl.multiple_of(j * tq, tq), tq, cr, parts=KEY_PARTS), carry)
    strip = tq // DIAG_STRIPS
    diag = pl.multiple_of(i * tq, tq)
    for n in range(DIAG_STRIPS):
        carry = fold(diag + n * strip, strip, carry, q_lo=n * strip, k_off=n * strip)
    _, l, acc = carry

    lp = lam_ref[0]
    lam = (jnp.exp(jnp.sum(lp[0:1] * lp[1:2], axis=-1, keepdims=True))
           - jnp.exp(jnp.sum(lp[2:3] * lp[3:4], axis=-1, keepdims=True)) + lambda_init)
    heads = []
    for hh in range(HEADS_PER_VREG):
        o = acc[2 * hh] * (1.0 / l[2 * hh]) - lam * (acc[2 * hh + 1] * (1.0 / l[2 * hh + 1]))
        ms = jnp.mean(o * o, axis=0, keepdims=True)
        heads.append(o * lax.rsqrt(ms + NORM_EPS) * sg_ref[0] * (1.0 - lambda_init))
    o_ref[0] = jnp.concatenate(heads, axis=0).T.astype(o_ref.dtype)


def _attention(qk3, vt, lam_params, subln_col, layer, lambda_init):
    b, s, _ = qk3.shape
    tq = ATT_TILE
    n_pairs = ATT_W // LANES
    return pl.pallas_call(
        functools.partial(_attn_kernel, lambda_init=lambda_init),
        grid=(b, n_pairs, s // tq),
        in_specs=[_layer_spec(lam_params.shape, layer), _layer_spec(subln_col.shape, layer),
                  pl.BlockSpec((1, tq, LANES), lambda bb, hp, i: (bb, i, hp)),
                  pl.BlockSpec((1, s, LANES), lambda bb, hp, i: (bb, 0, n_pairs + hp)),
                  pl.BlockSpec((LANES, s), lambda bb, hp, i: (hp, bb))],
        out_specs=pl.BlockSpec((1, tq, LANES), lambda bb, hp, i: (bb, i, hp)),
        out_shape=jax.ShapeDtypeStruct((b, s, ATT_W), BF16),
        compiler_params=_compiler_params(3),
        name="diff_attn",
    )(lam_params, subln_col, qk3, qk3, vt)


def _head_split(x, lane_head):
    return jnp.concatenate([jnp.where(lane_head == hh, x, jnp.zeros_like(x))
                            for hh in range(HEADS_PER_VREG)], axis=0)


def _tri_inverse_all(mats, r_idx, c_idx, lane_head):
    c = mats[0].shape[0]
    base = 16
    same = lambda blk: (r_idx // blk) == (c_idx // blk)
    eye = (r_idx == c_idx).astype(F32)
    mul = lambda ls, rs: [_mm(l, _head_split(r, lane_head)) for l, r in zip(ls, rs)]
    in_base = same(base)
    diag = [jnp.where(in_base, a, 0.0) for a in mats]
    powers = mul(diag, diag)
    invs = [eye + d for d in diag]
    span = 2
    while 2 * span < base:
        both = mul([jnp.concatenate([pw, inv], axis=0) for pw, inv in zip(powers, invs)], powers)
        powers = [t[:c] for t in both]
        invs = [inv + t[c:] for inv, t in zip(invs, both)]
        span *= 2
    invs = [inv + t for inv, t in zip(invs, mul(invs, powers))]
    blk = base
    while blk < c:
        off_diag = jnp.logical_and(same(2 * blk), jnp.logical_not(same(blk)))
        tmp = mul([jnp.where(off_diag, a, 0.0) for a in mats], invs)
        invs = [inv + t for inv, t in zip(invs, mul(invs, tmp))]
        blk *= 2
    return invs


def _rwkv_kernel(p_ref, mu_ref, vec_ref, wup_ref, aup_ref, gup_ref, tril_ref, hsum_ref,
                 o_ref, state_ref, prev_ref):
    nb, c, _ = p_ref.shape
    w = RWKV_W
    assert HEADS_PER_VREG * c == LANES

    @pl.when(pl.program_id(1) == 0)
    def _():
        state_ref[...] = jnp.zeros_like(state_ref)
        prev_ref[...] = jnp.zeros_like(prev_ref)

    p = jnp.concatenate([p_ref[n] for n in range(nb)], axis=0)
    row1 = lax.broadcasted_iota(jnp.int32, (nb * c, 1), 0)
    prev = pltpu.roll(p, 1, 0)
    for n in range(nb):
        prev = jnp.where(row1 == n * c, prev_ref[n, SUBLANES - 1:SUBLANES, :], prev)
        prev_ref[n] = p[(n + 1) * c - SUBLANES:(n + 1) * c, :]
    z = p + mu_ref[0] * (prev - p)

    vec = vec_ref[0]
    w0, a0, k_k, k_a, r_k, ln_g, ln_b = (vec[n:n + 1] for n in range(7))
    r = z[:, 0:w]
    k = z[:, w:2 * w]
    v = z[:, 2 * w:3 * w]
    lora_in = z[:, 3 * w:3 * w + DECAY_LORA + ICLR_LORA]
    g_in = z[:, 3 * w + DECAY_LORA + ICLR_LORA:]

    wdec = w0 + _mm(jnp.tanh(lora_in), wup_ref[0])
    logw = -(math.exp(-0.5) * math.log2(math.e)) * _sigmoid(wdec)
    iclr = _sigmoid(a0 + _mm(lora_in, aup_ref[0]))
    gate = _mm(_sigmoid(g_in), gup_ref[0])

    hsum = hsum_ref[...]
    kk = k * k_k
    kk = kk * lax.rsqrt(jnp.maximum(_head_sums(kk * kk, hsum, 1), 1e-24))
    k = k * (1.0 + (iclr - 1.0) * k_a)
    a_vec = -kk
    b_vec = kk * iclr
    lcum_all = _cumsum_rows(tril_ref[...], logw)

    r_idx = lax.broadcasted_iota(jnp.int32, (c, LANES), 0)
    lane = lax.broadcasted_iota(jnp.int32, (c, LANES), 1)
    c_idx = lane % c
    lane_head = lane // HEAD_DIM
    strict = c_idx < r_idx
    incl = c_idx <= r_idx
    strict2 = jnp.concatenate([strict, strict], axis=1)
    incl2 = jnp.concatenate([incl, incl], axis=1)
    blk_r = lax.broadcasted_iota(jnp.int32, (LANES, LANES), 0) // HEAD_DIM
    blk_c = lax.broadcasted_iota(jnp.int32, (LANES, LANES), 1) // HEAD_DIM
    same_head = blk_r == blk_c

    n_pairs = w // LANES
    groups = [(n, pr) for n in range(nb) for pr in range(n_pairs)]
    ar, bk_chk, bk_hat, vp, g_mid, g_end = [], [], [], [], [], []
    for n in range(nb):
        rows = slice(n * c, (n + 1) * c)
        lcum = lcum_all[rows]
        lmid = lcum[c // 2 - 1:c // 2, :]
        lend = lcum[c - 1:c, :]
        lc = lcum - lmid
        g_inv = jnp.exp2(-lc)
        g_tail = jnp.exp2(lend - lcum)
        a_til = a_vec[rows] * jnp.exp2(lc - logw[rows])
        r_til = r[rows] * jnp.exp2(lc)
        b_chk = b_vec[rows] * g_inv
        k_chk = k[rows] * g_inv
        b_hat = b_vec[rows] * g_tail
        k_hat = k[rows] * g_tail
        gm, ge = jnp.exp2(lmid), jnp.exp2(lend)
        for pr in range(n_pairs):
            sl = slice(pr * LANES, (pr + 1) * LANES)
            ar.append(jnp.concatenate([a_til[:, sl], r_til[:, sl]], axis=0).astype(BF16))
            bk_chk.append(jnp.concatenate([_head_split(b_chk[:, sl].astype(BF16), lane_head),
                                           _head_split(k_chk[:, sl].astype(BF16), lane_head)], axis=0))
            bk_hat.append(jnp.concatenate([b_hat[:, sl], k_hat[:, sl]], axis=0).astype(BF16))
            vp.append(v[rows, sl])
            g_mid.append(gm[:, sl])
            g_end.append(ge[:, sl])
    v_split = [_head_split(t.astype(BF16), lane_head) for t in vp]
    state = [state_ref[n, pr] for n, pr in groups]

    g = [_mm_nt(ar[gi], bk_chk[gi]) for gi in range(len(groups))]
    x = [_mm_nt(ar[gi], state[gi] * g_mid[gi]) for gi in range(len(groups))]
    w_a = [jnp.where(strict2, gh[:c], 0.0) for gh in g]
    w_r = [jnp.where(incl2, gh[c:], 0.0).astype(BF16) for gh in g]
    ak_v = [_mm(w_a[gi][:, LANES:], v_split[gi]) for gi in range(len(groups))]
    t_inv = _tri_inverse_all([wa[:, :LANES] for wa in w_a], r_idx, c_idx, lane_head)
    u = [_mm(t_inv[gi], _head_split(x[gi][:c] + ak_v[gi], lane_head)) for gi in range(len(groups))]
    y_pair = [x[gi][c:] + _mm(w_r[gi], jnp.concatenate(
                  [_head_split(u[gi].astype(BF16), lane_head), v_split[gi]], axis=0))
              for gi in range(len(groups))]
    for gi, (n, pr) in enumerate(groups):
        uv = jnp.concatenate([u[gi], vp[gi]], axis=0)
        state_ref[n, pr] = (state[gi] * g_end[gi]
                            + jnp.where(same_head, _mm_tn(uv, bk_hat[gi]), 0.0))
    y = jnp.concatenate([jnp.concatenate(y_pair[n * n_pairs:(n + 1) * n_pairs], axis=1)
                         for n in range(nb)], axis=0)
    mean = _head_sums(y, hsum, 2) * (1.0 / HEAD_DIM)
    yc = y - mean
    var = _head_sums(yc * yc, hsum, 1) * (1.0 / HEAD_DIM)
    yn = yc * lax.rsqrt(var + GN_EPS) * ln_g + ln_b
    bonus = _head_sums(r * k * r_k, hsum, 1) * v
    out = ((yn + bonus) * gate).astype(o_ref.dtype)
    for n in range(nb):
        o_ref[n] = out[n * c:(n + 1) * c]


def _rwkv(rw3, mu, vec, wup, aup, gup, tril, hsum, layer):
    b, s, cols = rw3.shape
    c = RWKV_CHUNK
    nb = RWKV_ROWS
    return pl.pallas_call(
        _rwkv_kernel,
        grid=(b // nb, s // c),
        in_specs=[pl.BlockSpec((nb, c, cols), lambda bb, i: (bb, i, 0)),
                  _layer_spec(mu.shape, layer), _layer_spec(vec.shape, layer),
                  _layer_spec(wup.shape, layer), _layer_spec(aup.shape, layer),
                  _layer_spec(gup.shape, layer), _const_spec(tril.shape), _const_spec(hsum.shape)],
        out_specs=pl.BlockSpec((nb, c, RWKV_W), lambda bb, i: (bb, i, 0)),
        out_shape=jax.ShapeDtypeStruct((b, s, RWKV_W), BF16),
        scratch_shapes=[pltpu.VMEM((nb, RWKV_W // LANES, LANES, LANES), F32),
                        pltpu.VMEM((nb, SUBLANES, cols), F32)],
        compiler_params=_compiler_params(2),
        name="rwkv7",
    )(rw3, mu, vec, wup, aup, gup, tril, hsum)


def _mlp_kernel(x_ref, att_ref, conv_ref, halo_ref, rw_ref, convw_ref, wo_ref, g_ref, wup_ref, wdn_ref,
                gfin_ref, o_ref, *, tiles_per_seq, final):
    tm = x_ref.shape[0]
    cw = CONV_W
    conv = conv_ref[...]
    zc = conv[:, cw:2 * cw] * conv[:, 2 * cw:]
    halo = halo_ref[...]
    seq_start = (pl.program_id(0) % tiles_per_seq) == 0
    zh = jnp.where(seq_start, 0.0, halo[:, cw:2 * cw] * halo[:, 2 * cw:])
    row = lax.broadcasted_iota(jnp.int32, (tm, 1), 0)
    z1 = jnp.where(row == 0, zh[SUBLANES - 1:SUBLANES], pltpu.roll(zc, 1, 0))
    z2 = jnp.where(row == 0, zh[SUBLANES - 2:SUBLANES - 1],
                   jnp.where(row == 1, zh[SUBLANES - 1:SUBLANES], pltpu.roll(zc, 2, 0)))
    cwt = convw_ref[0]
    o_conv = conv[:, :cw] * (cwt[0:1] * z2 + cwt[1:2] * z1 + cwt[2:3] * zc)

    x = x_ref[...]
    x = x + (jnp.dot(att_ref[...], wo_ref[0, :ATT_W], preferred_element_type=F32)
             + jnp.dot(o_conv.astype(BF16), wo_ref[0, ATT_W:ATT_W + cw], preferred_element_type=F32)
             + jnp.dot(rw_ref[...], wo_ref[0, ATT_W + cw:], preferred_element_type=F32))

    h = _rms(x, g_ref[0]).astype(BF16)
    d_ff = wup_ref.shape[2]
    acc = x
    for j in range(d_ff // FF_CHUNK):
        cols = slice(j * FF_CHUNK, (j + 1) * FF_CHUNK)
        up = jnp.maximum(jnp.dot(h, wup_ref[0, :, cols], preferred_element_type=F32), 0.0)
        acc = acc + jnp.dot((up * up).astype(BF16), wdn_ref[0, cols, :], preferred_element_type=F32)
    if final:
        acc = _rms(acc, gfin_ref[...])
    o_ref[...] = acc


def _mlp(x2, att2, conv2, rw2, convw, wo, g, wup, wdn, gfin, *, layer, seq, final):
    t, d = x2.shape
    tm = ROW_TILE
    halo_blocks = tm // SUBLANES
    row = lambda n: pl.BlockSpec((tm, n), lambda i: (i, 0))
    halo = pl.BlockSpec((SUBLANES, conv2.shape[1]),
                        lambda i: (jnp.maximum(i * halo_blocks - 1, 0), 0))
    params = [convw, wo, g, wup, wdn]
    return pl.pallas_call(
        functools.partial(_mlp_kernel, tiles_per_seq=seq // tm, final=final),
        grid=(t // tm,),
        in_specs=[row(d), row(att2.shape[1]), row(conv2.shape[1]), halo, row(rw2.shape[1])]
                 + [_layer_spec(a.shape, layer) for a in params] + [_const_spec(gfin.shape)],
        out_specs=row(d),
        out_shape=jax.ShapeDtypeStruct((t, d), F32),
        compiler_params=_compiler_params(1),
        name="mix_mlp",
    )(x2, att2, conv2, conv2, rw2, *params, gfin)


def kernel(x, norm_mix_g, w_in, lam_q1, lam_k1, lam_q2, lam_k2, subln_g, conv_w, shift_mu, rwkv_w0, rwkv_w_up, rwkv_a0, rwkv_a_up, rwkv_g_up, rwkv_k_k, rwkv_k_a, rwkv_r_k, lnx_g, lnx_b, w_out, norm_mlp_g, w_mlp_up, w_mlp_down, final_norm_g):
    bsz, seq, d = x.shape
    depth = w_in.shape[0]
    t = bsz * seq
    assert t % ROW_TILE == 0 and seq % ROW_TILE == 0 and seq % ATT_TILE == 0 and seq % RWKV_CHUNK == 0
    assert bsz % RWKV_ROWS == 0

    att_cols = 3 * ATT_W
    qscale = jnp.concatenate([jnp.full((1, ATT_W), DIFF_HALF ** -0.5 * math.log2(math.e), F32),
                              jnp.ones((1, ATT_W), F32)], axis=1)
    idx = jnp.arange(min(CUMSUM_BLOCK, RWKV_ROWS * RWKV_CHUNK))
    tril = jnp.logical_and(idx[None, :] <= idx[:, None],
                           idx[None, :] // RWKV_CHUNK == idx[:, None] // RWKV_CHUNK).astype(BF16)
    hidx = jnp.arange(LANES) // HEAD_DIM
    hsum = (hidx[:, None] == hidx[None, :]).astype(BF16)
    w_out_bf = w_out.astype(BF16)
    lam_params = jnp.stack([lam_q1, lam_k1, lam_q2, lam_k2], axis=1)
    vec = jnp.stack([rwkv_w0, rwkv_a0, rwkv_k_k, rwkv_k_a, rwkv_r_k.reshape(depth, RWKV_W), lnx_g, lnx_b,
                     jnp.zeros((depth, RWKV_W), F32)], axis=1)
    lora_pad = jnp.zeros((depth, DECAY_LORA, RWKV_W), F32)
    wup = jnp.concatenate([rwkv_w_up, lora_pad], axis=1).astype(BF16)
    aup = jnp.concatenate([lora_pad, rwkv_a_up], axis=1).astype(BF16)
    gup = rwkv_g_up.astype(BF16)

    x2 = x.reshape(t, d)
    for l in range(depth):
        lambda_init = 0.8 - 0.6 * math.exp(-0.3 * l)
        w_in_bf = w_in[l:l + 1].astype(BF16)
        w_vt_bf = jnp.swapaxes(w_in_bf[:, :, 2 * ATT_W:att_cols], 1, 2)
        qk2, vt, conv2, rw2 = _inproj(x2, norm_mix_g[:, None], qscale, w_in_bf, w_vt_bf, l)
        o_att = _attention(qk2.reshape(bsz, seq, 2 * ATT_W), vt, lam_params, subln_g[:, :, None], l,
                           lambda_init)
        o_rw = _rwkv(rw2.reshape(bsz, seq, RWKV_COLS), shift_mu[:, None], vec, wup, aup, gup, tril, hsum, l)
        x2 = _mlp(x2, o_att.reshape(t, ATT_W), conv2, o_rw.reshape(t, RWKV_W), conv_w, w_out_bf,
                  norm_mlp_g[:, None], w_mlp_up[l:l + 1].astype(BF16), w_mlp_down[l:l + 1].astype(BF16),
                  final_norm_g[None],
                  layer=l, seq=seq, final=(l == depth - 1))
    return x2.reshape(bsz, seq, d)
```

```python
import functools
import math

import jax
import jax.numpy as jnp
from jax import lax
from jax.experimental import pallas as pl
from jax.experimental.pallas import tpu as pltpu

F32 = jnp.float32
BF16 = jnp.bfloat16

HEAD_DIM = 64
DIFF_HALF = HEAD_DIM // 2
ATT_HEADS = 6
CONV_GROUPS = 4
RWKV_HEADS = 6
ATT_W = ATT_HEADS * HEAD_DIM
CONV_W = CONV_GROUPS * HEAD_DIM
RWKV_W = RWKV_HEADS * HEAD_DIM
CONV_K = 3
DECAY_LORA = 64
ICLR_LORA = 64
GATE_LORA = 128
RWKV_COLS = 3 * RWKV_W + DECAY_LORA + ICLR_LORA + GATE_LORA
NORM_EPS = 1e-6
GN_EPS = 64e-5

LANES = 128
SUBLANES = 8
HEADS_PER_VREG = LANES // HEAD_DIM
VMEM_LIMIT_BYTES = 56 * 1024 * 1024

ROW_TILE = 512
ATT_TILE = 1024
KEY_PARTS = 4
DIAG_STRIPS = 4
RWKV_CHUNK = 64
RWKV_ROWS = 8
TRI_BASE = 16
CUMSUM_BLOCK = 256
FF_CHUNK = 1024
NEG_BIG = -1e30
ONES_ROWS = 16


def _compiler_params(n_axes):
    return pltpu.CompilerParams(
        dimension_semantics=("arbitrary",) * n_axes,
        vmem_limit_bytes=VMEM_LIMIT_BYTES,
    )


def _const_spec(shape):
    nd = len(shape)
    return pl.BlockSpec(shape, lambda *_: (0,) * nd)


def _layer_spec(shape, layer):
    nd = len(shape)
    index = layer if shape[0] > 1 else 0
    return pl.BlockSpec((1,) + tuple(shape[1:]), lambda *_: (index,) + (0,) * (nd - 1))


def _mm(a, b):
    return jnp.dot(a.astype(BF16), b.astype(BF16), preferred_element_type=F32)


def _mm_nt(a, b):
    return lax.dot_general(a.astype(BF16), b.astype(BF16), (((1,), (1,)), ((), ())),
                           preferred_element_type=F32)


def _mm_tn(a, b):
    return lax.dot_general(a.astype(BF16), b.astype(BF16), (((0,), (0,)), ((), ())),
                           preferred_element_type=F32)


def _bf16_pieces(x, terms):
    pieces = []
    rem = x
    for t in range(terms):
        piece = rem.astype(BF16)
        pieces.append(piece)
        if t + 1 < terms:
            rem = rem - piece.astype(F32)
    return pieces


def _head_sums(x, ones_bf16, terms):
    out = []
    for s in range(x.shape[1] // LANES):
        xs = x[:, s * LANES:(s + 1) * LANES]
        out.append(functools.reduce(lambda a, b: a + b, [
            jnp.dot(piece, ones_bf16, preferred_element_type=F32) for piece in _bf16_pieces(xs, terms)]))
    return jnp.concatenate(out, axis=1)


def _cumsum_rows(tril_bf16, x):
    blk = tril_bf16.shape[0]
    out = []
    for s in range(x.shape[0] // blk):
        xs = x[s * blk:(s + 1) * blk]
        out.append(functools.reduce(lambda a, b: a + b, [
            jnp.dot(tril_bf16, piece, preferred_element_type=F32) for piece in _bf16_pieces(xs, 3)]))
    return jnp.concatenate(out, axis=0)


def _sigmoid(x):
    return 0.5 * jnp.tanh(0.5 * x) + 0.5


def _rms(x, g):
    ms = jnp.mean(x * x, axis=-1, keepdims=True)
    return x * lax.rsqrt(ms + NORM_EPS) * g


def _inproj_kernel(x_ref, g_ref, qscale_ref, w_ref, w_vt_ref, qk_ref, vt_ref, conv_ref, rw_ref):
    h = _rms(x_ref[...], g_ref[0]).astype(BF16)
    qk_end, v_end, conv_end = 2 * ATT_W, 3 * ATT_W, 3 * ATT_W + 3 * CONV_W
    qk = jnp.dot(h, w_ref[0, :, :qk_end], preferred_element_type=F32)
    qk_ref[...] = (qk * qscale_ref[...]).astype(BF16)
    vt_ref[...] = _mm_nt(w_vt_ref[0], h).astype(BF16)
    conv_ref[...] = jnp.dot(h, w_ref[0, :, v_end:conv_end], preferred_element_type=F32)
    rw_ref[...] = jnp.dot(h, w_ref[0, :, conv_end:], preferred_element_type=F32)


def _inproj(x2, g, qscale, w_in, w_vt, layer):
    t, d = x2.shape
    tm = ROW_TILE
    row = lambda n: pl.BlockSpec((tm, n), lambda i: (i, 0))
    return pl.pallas_call(
        _inproj_kernel,
        grid=(t // tm,),
        in_specs=[row(d), _layer_spec(g.shape, layer), _const_spec(qscale.shape),
                  _layer_spec(w_in.shape, layer), _layer_spec(w_vt.shape, layer)],
        out_specs=[row(2 * ATT_W), pl.BlockSpec((ATT_W, tm), lambda i: (0, i)),
                   row(3 * CONV_W), row(RWKV_COLS)],
        out_shape=[jax.ShapeDtypeStruct((t, 2 * ATT_W), BF16),
                   jax.ShapeDtypeStruct((ATT_W, t), BF16),
                   jax.ShapeDtypeStruct((t, 3 * CONV_W), F32),
                   jax.ShapeDtypeStruct((t, RWKV_COLS), F32)],
        compiler_params=_compiler_params(1),
        name="inproj",
    )(x2, g, qscale, w_in, w_vt)


def _attn_kernel(lam_ref, sg_ref, q_ref, k_ref, vt_ref, o_ref, *, lambda_init):
    tq = q_ref.shape[1]
    i = pl.program_id(2)
    q = q_ref[0]
    lane = lax.broadcasted_iota(jnp.int32, (tq, LANES), 1)
    n_maps = LANES // DIFF_HALF
    maps = range(n_maps)
    q_sel = [jnp.where(lane // DIFF_HALF == c, q, jnp.zeros_like(q)) for c in maps]

    def scores(k_start, tk, q_lo=0, k_off=None):
        ks = k_ref[0, pl.ds(k_start, tk), :]
        out = [_mm_nt(ks, q_sel[c][q_lo:]) for c in maps]
        if k_off is not None:
            key_pos = lax.broadcasted_iota(jnp.int32, (tk, tq - q_lo), 0) + k_off
            qry_pos = lax.broadcasted_iota(jnp.int32, (tk, tq - q_lo), 1) + q_lo
            out = [jnp.where(key_pos <= qry_pos, s, -jnp.inf) for s in out]
        return out

    def softmax(s, m):
        m_new = [jnp.maximum(m[c], jnp.max(s[c], axis=0, keepdims=True)) for c in maps]
        alpha = [jnp.exp2(m[c] - m_new[c]) for c in maps]
        p = [jnp.exp2(s[c] - m_new[c]).astype(BF16) for c in maps]
        return m_new, alpha, p

    def values(k_start, tk, p):
        vt = vt_ref[:, pl.ds(k_start, tk)]
        ones = jnp.ones((ONES_ROWS, tk), BF16)
        vt_aug = [jnp.concatenate([vt[hh * HEAD_DIM:(hh + 1) * HEAD_DIM], ones], axis=0)
                  for hh in range(HEADS_PER_VREG)]
        return [jnp.dot(vt_aug[c // 2], p[c], preferred_element_type=F32) for c in maps]

    def absorb(s, k_start, tk, carry, q_lo=0):
        m, l, acc = carry
        m_new, alpha, p = softmax(s, [m[c][:, q_lo:] for c in maps])
        pv = values(k_start, tk, p)
        l_new = [alpha[c] * l[c][:, q_lo:] + pv[c][HEAD_DIM:HEAD_DIM + 1] for c in maps]
        acc_new = [alpha[c] * acc[c][:, q_lo:] + pv[c][:HEAD_DIM] for c in maps]
        if q_lo:
            keep = lambda old, new: [jnp.concatenate([old[c][:, :q_lo], new[c]], axis=1) for c in maps]
            m_new, l_new, acc_new = keep(m, m_new), keep(l, l_new), keep(acc, acc_new)
        return m_new, l_new, acc_new

    def trip(j, carry):
        start = pl.multiple_of(j * tq, tq)
        sub = tq // KEY_PARTS
        s_parts = [scores(start + n * sub, sub) for n in range(KEY_PARTS)]
        for n in range(KEY_PARTS):
            carry = absorb(s_parts[n], start + n * sub, sub, carry)
        return carry

    carry = ([jnp.full((1, tq), NEG_BIG, F32) for _ in maps], [jnp.zeros((1, tq), F32) for _ in maps],
             [jnp.zeros((HEAD_DIM, tq), F32) for _ in maps])
    carry = lax.fori_loop(0, i, trip, carry)
    strip = tq // DIAG_STRIPS
    diag = pl.multiple_of(i * tq, tq)
    s_strips = [scores(diag + n * strip, strip, q_lo=n * strip, k_off=n * strip)
                for n in range(DIAG_STRIPS)]
    for n in range(DIAG_STRIPS):
        carry = absorb(s_strips[n], diag + n * strip, strip, carry, q_lo=n * strip)
    _, l, acc = carry

    lp = lam_ref[0]
    lam = (jnp.exp(jnp.sum(lp[0:1] * lp[1:2], axis=-1, keepdims=True))
           - jnp.exp(jnp.sum(lp[2:3] * lp[3:4], axis=-1, keepdims=True)) + lambda_init)
    heads = []
    for hh in range(HEADS_PER_VREG):
        o = acc[2 * hh] * (1.0 / l[2 * hh]) - lam * (acc[2 * hh + 1] * (1.0 / l[2 * hh + 1]))
        ms = jnp.mean(o * o, axis=0, keepdims=True)
        heads.append(o * lax.rsqrt(ms + NORM_EPS) * sg_ref[0] * (1.0 - lambda_init))
    o_ref[0] = jnp.concatenate(heads, axis=0).T.astype(o_ref.dtype)


def _attention(qk3, vt, lam_params, subln_col, layer, lambda_init):
    b, s, _ = qk3.shape
    tq = ATT_TILE
    n_pairs = ATT_W // LANES
    return pl.pallas_call(
        functools.partial(_attn_kernel, lambda_init=lambda_init),
        grid=(b, n_pairs, s // tq),
        in_specs=[_layer_spec(lam_params.shape, layer), _layer_spec(subln_col.shape, layer),
                  pl.BlockSpec((1, tq, LANES), lambda bb, hp, i: (bb, i, hp)),
                  pl.BlockSpec((1, s, LANES), lambda bb, hp, i: (bb, 0, n_pairs + hp)),
                  pl.BlockSpec((LANES, s), lambda bb, hp, i: (hp, bb))],
        out_specs=pl.BlockSpec((1, tq, LANES), lambda bb, hp, i: (bb, i, hp)),
        out_shape=jax.ShapeDtypeStruct((b, s, ATT_W), BF16),
        compiler_params=_compiler_params(3),
        name="diff_attn",
    )(lam_params, subln_col, qk3, qk3, vt)


def _head_split(x, lane_head):
    return jnp.concatenate([jnp.where(lane_head == hh, x, jnp.zeros_like(x))
                            for hh in range(HEADS_PER_VREG)], axis=0)


def _tri_inverse_all(mats, r_idx, c_idx, lane_head):
    c = mats[0].shape[0]
    base = TRI_BASE
    same = lambda blk: (r_idx // blk) == (c_idx // blk)
    eye = (r_idx == c_idx).astype(F32)
    mul = lambda ls, rs: [_mm(l, _head_split(r, lane_head)) for l, r in zip(ls, rs)]
    in_base = same(base)
    diag = [jnp.where(in_base, a, 0.0) for a in mats]
    powers = mul(diag, diag)
    invs = [eye + d for d in diag]
    span = 2
    while 2 * span < base:
        both = mul([jnp.concatenate([pw, inv], axis=0) for pw, inv in zip(powers, invs)], powers)
        powers = [t[:c] for t in both]
        invs = [inv + t[c:] for inv, t in zip(invs, both)]
        span *= 2
    invs = [inv + t for inv, t in zip(invs, mul(invs, powers))]
    blk = base
    while blk < c:
        off_diag = jnp.logical_and(same(2 * blk), jnp.logical_not(same(blk)))
        tmp = mul([jnp.where(off_diag, a, 0.0) for a in mats], invs)
        invs = [inv + t for inv, t in zip(invs, mul(invs, tmp))]
        blk *= 2
    return invs


def _rwkv_kernel(p_ref, mu_ref, vec_ref, wup_ref, aup_ref, gup_ref, tril_ref, hsum_ref,
                 o_ref, state_ref, prev_ref):
    nb, c, _ = p_ref.shape
    w = RWKV_W
    assert HEADS_PER_VREG * c == LANES

    @pl.when(pl.program_id(1) == 0)
    def _():
        state_ref[...] = jnp.zeros_like(state_ref)
        prev_ref[...] = jnp.zeros_like(prev_ref)

    p = jnp.concatenate([p_ref[n] for n in range(nb)], axis=0)
    row1 = lax.broadcasted_iota(jnp.int32, (nb * c, 1), 0)
    prev = pltpu.roll(p, 1, 0)
    for n in range(nb):
        prev = jnp.where(row1 == n * c, prev_ref[n, SUBLANES - 1:SUBLANES, :], prev)
        prev_ref[n] = p[(n + 1) * c - SUBLANES:(n + 1) * c, :]
    z = p + mu_ref[0] * (prev - p)

    vec = vec_ref[0]
    w0, a0, k_k, k_a, r_k, ln_g, ln_b = (vec[n:n + 1] for n in range(7))
    r = z[:, 0:w]
    k = z[:, w:2 * w]
    v = z[:, 2 * w:3 * w]
    lora_in = z[:, 3 * w:3 * w + DECAY_LORA + ICLR_LORA]
    g_in = z[:, 3 * w + DECAY_LORA + ICLR_LORA:]

    wdec = w0 + _mm(jnp.tanh(lora_in), wup_ref[0])
    logw = -(math.exp(-0.5) * math.log2(math.e)) * _sigmoid(wdec)
    iclr = _sigmoid(a0 + _mm(lora_in, aup_ref[0]))
    gate = _mm(_sigmoid(g_in), gup_ref[0])

    hsum = hsum_ref[...]
    kk = k * k_k
    kk = kk * lax.rsqrt(jnp.maximum(_head_sums(kk * kk, hsum, 1), 1e-24))
    k = k * (1.0 + (iclr - 1.0) * k_a)
    a_vec = -kk
    b_vec = kk * iclr
    lcum_all = _cumsum_rows(tril_ref[...], logw)

    r_idx = lax.broadcasted_iota(jnp.int32, (c, LANES), 0)
    lane = lax.broadcasted_iota(jnp.int32, (c, LANES), 1)
    c_idx = lane % c
    lane_head = lane // HEAD_DIM
    strict = c_idx < r_idx
    incl = c_idx <= r_idx
    strict2 = jnp.concatenate([strict, strict], axis=1)
    incl2 = jnp.concatenate([incl, incl], axis=1)
    blk_r = lax.broadcasted_iota(jnp.int32, (LANES, LANES), 0) // HEAD_DIM
    blk_c = lax.broadcasted_iota(jnp.int32, (LANES, LANES), 1) // HEAD_DIM
    same_head = blk_r == blk_c

    n_pairs = w // LANES
    groups = [(n, pr) for n in range(nb) for pr in range(n_pairs)]
    ar, bk_chk, bk_hat, vp, g_mid, g_end = [], [], [], [], [], []
    for n in range(nb):
        rows = slice(n * c, (n + 1) * c)
        lcum = lcum_all[rows]
        lmid = lcum[c // 2 - 1:c // 2, :]
        lend = lcum[c - 1:c, :]
        lc = lcum - lmid
        g_inv = jnp.exp2(-lc)
        g_tail = jnp.exp2(lend - lcum)
        a_til = a_vec[rows] * jnp.exp2(lc - logw[rows])
        r_til = r[rows] * jnp.exp2(lc)
        b_chk = b_vec[rows] * g_inv
        k_chk = k[rows] * g_inv
        b_hat = b_vec[rows] * g_tail
        k_hat = k[rows] * g_tail
        gm, ge = jnp.exp2(lmid), jnp.exp2(lend)
        for pr in range(n_pairs):
            sl = slice(pr * LANES, (pr + 1) * LANES)
            ar.append(jnp.concatenate([a_til[:, sl], r_til[:, sl]], axis=0).astype(BF16))
            bk_chk.append(jnp.concatenate([_head_split(b_chk[:, sl].astype(BF16), lane_head),
                                           _head_split(k_chk[:, sl].astype(BF16), lane_head)], axis=0))
            bk_hat.append(jnp.concatenate([b_hat[:, sl], k_hat[:, sl]], axis=0).astype(BF16))
            vp.append(v[rows, sl])
            g_mid.append(gm[:, sl])
            g_end.append(ge[:, sl])
    v_split = [_head_split(t.astype(BF16), lane_head) for t in vp]
    state = [state_ref[n, pr] for n, pr in groups]

    g = [_mm_nt(ar[gi], bk_chk[gi]) for gi in range(len(groups))]
    x = [_mm_nt(ar[gi], state[gi] * g_mid[gi]) for gi in range(len(groups))]
    w_a = [jnp.where(strict2, gh[:c], 0.0) for gh in g]
    w_r = [jnp.where(incl2, gh[c:], 0.0).astype(BF16) for gh in g]
    ak_v = [_mm(w_a[gi][:, LANES:], v_split[gi]) for gi in range(len(groups))]
    t_inv = _tri_inverse_all([wa[:, :LANES] for wa in w_a], r_idx, c_idx, lane_head)
    u = [_mm(t_inv[gi], _head_split(x[gi][:c] + ak_v[gi], lane_head)) for gi in range(len(groups))]
    y_pair = [x[gi][c:] + _mm(w_r[gi], jnp.concatenate(
                  [_head_split(u[gi].astype(BF16), lane_head), v_split[gi]], axis=0))
              for gi in range(len(groups))]
    for gi, (n, pr) in enumerate(groups):
        uv = jnp.concatenate([u[gi], vp[gi]], axis=0)
        state_ref[n, pr] = (state[gi] * g_end[gi]
                            + jnp.where(same_head, _mm_tn(uv, bk_hat[gi]), 0.0))
    y = jnp.concatenate([jnp.concatenate(y_pair[n * n_pairs:(n + 1) * n_pairs], axis=1)
                         for n in range(nb)], axis=0)
    mean = _head_sums(y, hsum, 2) * (1.0 / HEAD_DIM)
    yc = y - mean
    var = _head_sums(yc * yc, hsum, 1) * (1.0 / HEAD_DIM)
    yn = yc * lax.rsqrt(var + GN_EPS) * ln_g + ln_b
    bonus = _head_sums(r * k * r_k, hsum, 1) * v
    out = ((yn + bonus) * gate).astype(o_ref.dtype)
    for n in range(nb):
        o_ref[n] = out[n * c:(n + 1) * c]


def _rwkv(rw3, mu, vec, wup, aup, gup, tril, hsum, layer):
    b, s, cols = rw3.shape
    c = RWKV_CHUNK
    nb = RWKV_ROWS
    return pl.pallas_call(
        _rwkv_kernel,
        grid=(b // nb, s // c),
        in_specs=[pl.BlockSpec((nb, c, cols), lambda bb, i: (bb, i, 0)),
                  _layer_spec(mu.shape, layer), _layer_spec(vec.shape, layer),
                  _layer_spec(wup.shape, layer), _layer_spec(aup.shape, layer),
                  _layer_spec(gup.shape, layer), _const_spec(tril.shape), _const_spec(hsum.shape)],
        out_specs=pl.BlockSpec((nb, c, RWKV_W), lambda bb, i: (bb, i, 0)),
        out_shape=jax.ShapeDtypeStruct((b, s, RWKV_W), BF16),
        scratch_shapes=[pltpu.VMEM((nb, RWKV_W // LANES, LANES, LANES), F32),
                        pltpu.VMEM((nb, SUBLANES, cols), F32)],
        compiler_params=_compiler_params(2),
        name="rwkv7",
    )(rw3, mu, vec, wup, aup, gup, tril, hsum)


def _mlp_kernel(x_ref, att_ref, conv_ref, halo_ref, rw_ref, convw_ref, wo_ref, g_ref, wup_ref, wdn_ref,
                gfin_ref, o_ref, *, tiles_per_seq, final):
    tm = x_ref.shape[0]
    cw = CONV_W
    conv = conv_ref[...]
    zc = conv[:, cw:2 * cw] * conv[:, 2 * cw:]
    halo = halo_ref[...]
    seq_start = (pl.program_id(0) % tiles_per_seq) == 0
    zh = jnp.where(seq_start, 0.0, halo[:, cw:2 * cw] * halo[:, 2 * cw:])
    row = lax.broadcasted_iota(jnp.int32, (tm, 1), 0)
    z1 = jnp.where(row == 0, zh[SUBLANES - 1:SUBLANES], pltpu.roll(zc, 1, 0))
    z2 = jnp.where(row == 0, zh[SUBLANES - 2:SUBLANES - 1],
                   jnp.where(row == 1, zh[SUBLANES - 1:SUBLANES], pltpu.roll(zc, 2, 0)))
    cwt = convw_ref[0]
    o_conv = conv[:, :cw] * (cwt[0:1] * z2 + cwt[1:2] * z1 + cwt[2:3] * zc)

    x = x_ref[...]
    x = x + (jnp.dot(att_ref[...], wo_ref[0, :ATT_W], preferred_element_type=F32)
             + jnp.dot(o_conv.astype(BF16), wo_ref[0, ATT_W:ATT_W + cw], preferred_element_type=F32)
             + jnp.dot(rw_ref[...], wo_ref[0, ATT_W + cw:], preferred_element_type=F32))

    h = _rms(x, g_ref[0]).astype(BF16)
    d_ff = wup_ref.shape[2]
    acc = x
    for j in range(d_ff // FF_CHUNK):
        cols = slice(j * FF_CHUNK, (j + 1) * FF_CHUNK)
        up = jnp.maximum(jnp.dot(h, wup_ref[0, :, cols], preferred_element_type=F32), 0.0)
        acc = acc + jnp.dot((up * up).astype(BF16), wdn_ref[0, cols, :], preferred_element_type=F32)
    if final:
        acc = _rms(acc, gfin_ref[...])
    o_ref[...] = acc


def _mlp(x2, att2, conv2, rw2, convw, wo, g, wup, wdn, gfin, *, layer, seq, final):
    t, d = x2.shape
    tm = ROW_TILE
    halo_blocks = tm // SUBLANES
    row = lambda n: pl.BlockSpec((tm, n), lambda i: (i, 0))
    halo = pl.BlockSpec((SUBLANES, conv2.shape[1]),
                        lambda i: (jnp.maximum(i * halo_blocks - 1, 0), 0))
    params = [convw, wo, g, wup, wdn]
    return pl.pallas_call(
        functools.partial(_mlp_kernel, tiles_per_seq=seq // tm, final=final),
        grid=(t // tm,),
        in_specs=[row(d), row(att2.shape[1]), row(conv2.shape[1]), halo, row(rw2.shape[1])]
                 + [_layer_spec(a.shape, layer) for a in params] + [_const_spec(gfin.shape)],
        out_specs=row(d),
        out_shape=jax.ShapeDtypeStruct((t, d), F32),
        compiler_params=_compiler_params(1),
        name="mix_mlp",
    )(x2, att2, conv2, conv2, rw2, *params, gfin)


def kernel(x, norm_mix_g, w_in, lam_q1, lam_k1, lam_q2, lam_k2, subln_g, conv_w, shift_mu, rwkv_w0, rwkv_w_up, rwkv_a0, rwkv_a_up, rwkv_g_up, rwkv_k_k, rwkv_k_a, rwkv_r_k, lnx_g, lnx_b, w_out, norm_mlp_g, w_mlp_up, w_mlp_down, final_norm_g):
    bsz, seq, d = x.shape
    depth = w_in.shape[0]
    t = bsz * seq
    assert t % ROW_TILE == 0 and seq % ROW_TILE == 0 and seq % ATT_TILE == 0 and seq % RWKV_CHUNK == 0
    assert bsz % RWKV_ROWS == 0

    att_cols = 3 * ATT_W
    qscale = jnp.concatenate([jnp.full((1, ATT_W), DIFF_HALF ** -0.5 * math.log2(math.e), F32),
                              jnp.ones((1, ATT_W), F32)], axis=1)
    idx = jnp.arange(min(CUMSUM_BLOCK, RWKV_ROWS * RWKV_CHUNK))
    tril = jnp.logical_and(idx[None, :] <= idx[:, None],
                           idx[None, :] // RWKV_CHUNK == idx[:, None] // RWKV_CHUNK).astype(BF16)
    hidx = jnp.arange(LANES) // HEAD_DIM
    hsum = (hidx[:, None] == hidx[None, :]).astype(BF16)
    w_in_bf = w_in.astype(BF16)
    w_vt_bf = jnp.swapaxes(w_in_bf[:, :, 2 * ATT_W:att_cols], 1, 2)
    w_out_bf = w_out.astype(BF16)
    lam_params = jnp.stack([lam_q1, lam_k1, lam_q2, lam_k2], axis=1)
    vec = jnp.stack([rwkv_w0, rwkv_a0, rwkv_k_k, rwkv_k_a, rwkv_r_k.reshape(depth, RWKV_W), lnx_g, lnx_b,
                     jnp.zeros((depth, RWKV_W), F32)], axis=1)
    lora_pad = jnp.zeros((depth, DECAY_LORA, RWKV_W), F32)
    wup = jnp.concatenate([rwkv_w_up, lora_pad], axis=1).astype(BF16)
    aup = jnp.concatenate([lora_pad, rwkv_a_up], axis=1).astype(BF16)
    gup = rwkv_g_up.astype(BF16)

    x2 = x.reshape(t, d)
    for l in range(depth):
        lambda_init = 0.8 - 0.6 * math.exp(-0.3 * l)
        qk2, vt, conv2, rw2 = _inproj(x2, norm_mix_g[:, None], qscale, w_in_bf, w_vt_bf, l)
        o_att = _attention(qk2.reshape(bsz, seq, 2 * ATT_W), vt, lam_params, subln_g[:, :, None], l,
                           lambda_init)
        o_rw = _rwkv(rw2.reshape(bsz, seq, RWKV_COLS), shift_mu[:, None], vec, wup, aup, gup, tril, hsum, l)
        x2 = _mlp(x2, o_att.reshape(t, ATT_W), conv2, o_rw.reshape(t, RWKV_W), conv_w, w_out_bf,
                  norm_mlp_g[:, None], w_mlp_up[l:l + 1].astype(BF16), w_mlp_down[l:l + 1].astype(BF16),
                  final_norm_g[None],
                  layer=l, seq=seq, final=(l == depth - 1))
    return x2.reshape(bsz, seq, d)
```
